```python
import math
import jax, jax.numpy as jnp
from jax import lax
import numpy as np

D_MODEL = 1024
BATCH = 2
SEQ = 8192
DEPTH = 4

GRID_W = 64
CTX_LEN = 256
N_MIXERS = 3
N_A = (DEPTH + 2) // 3
N_B = (DEPTH + 1) // 3
N_C = DEPTH // 3
N_MOD = 9
EPS = 1e-6

FFN_DIM = 2816

SSD_EXPAND = 2
SSD_INNER = SSD_EXPAND * D_MODEL
SSD_HEADDIM = 64
SSD_HEADS = SSD_INNER // SSD_HEADDIM
SSD_STATE = 128
SSD_GROUPS = 4
SSD_HPG = SSD_HEADS // SSD_GROUPS
SSD_CONV = 5
SSD_CHUNK = 128
SSD_CONV_DIM = SSD_INNER + 2 * SSD_GROUPS * SSD_STATE
SSD_PROJ = 2 * SSD_INNER + 2 * SSD_GROUPS * SSD_STATE + 2 * SSD_HEADS

SC_WIDTH = 3

GM_CHUNK = 128
GM_DIM = 2 * D_MODEL
GM_GROUPS = 8
GM_GROUP_DIM = GM_DIM // GM_GROUPS

kernel_name = "hybrid_ssd_shortconv_chunkmlp_macaron_prefix"


def rmsnorm(x, g):
    xf = x.astype(jnp.float32)
    y = xf * lax.rsqrt(jnp.mean(xf * xf, axis=-1, keepdims=True) + EPS)
    return (y * g.astype(jnp.float32)).astype(x.dtype)


def layernorm(x, g):
    xf = x.astype(jnp.float32)
    xc = xf - jnp.mean(xf, axis=-1, keepdims=True)
    y = xc * lax.rsqrt(jnp.mean(xc * xc, axis=-1, keepdims=True) + EPS)
    return (y * g.astype(jnp.float32)).astype(x.dtype)


def modulate(x, shift, scale):
    return x * (1 + scale) + shift


def half_ffn(h, shift, scale, gate, g, wg, wu, wd):
    u = modulate(rmsnorm(h, g), shift, scale)
    return h + 0.5 * gate * ((jax.nn.silu(u @ wg) * (u @ wu)) @ wd)


def dwconv(x, w):
    k = w.shape[0]
    return lax.conv_general_dilated(
        x, w[:, None, :], window_strides=(1,), padding=[(k // 2, k // 2)],
        dimension_numbers=("NWC", "WIO", "NWC"), feature_group_count=x.shape[-1])


def ssd_scan(x, dt, a, bm, cm, s0):
    f32 = jnp.float32
    bsz, t, h, p = x.shape
    g, n = bm.shape[2], bm.shape[3]
    hg = h // g
    q = SSD_CHUNK
    nc = t // q
    xc = x.astype(f32).reshape(bsz, nc, q, g, hg, p)
    dtc = dt.astype(f32).reshape(bsz, nc, q, g, hg)
    bc = bm.astype(f32).reshape(bsz, nc, q, g, n)
    cc = cm.astype(f32).reshape(bsz, nc, q, g, n)
    cs = jnp.cumsum(dtc * a.astype(f32).reshape(g, hg), axis=2)
    xdt = xc * dtc[..., None]
    seg = cs[:, :, :, None] - cs[:, :, None, :]
    lower = jnp.tril(jnp.ones((q, q), bool))[None, None, :, :, None, None]
    decay = jnp.exp(jnp.where(lower, seg, -jnp.inf))
    cb = jnp.einsum('bcign,bcjgn->bcijg', cc, bc)
    y_diag = jnp.einsum('bcijgh,bcjghp->bcighp', decay * cb[..., None], xdt)
    to_end = jnp.exp(cs[:, :, -1:] - cs)
    states = jnp.einsum('bcjgn,bcjghp->bcghpn', bc, xdt * to_end[..., None])
    chunk_decay = jnp.exp(cs[:, :, -1])

    def step(s, inp):
        st, dec = inp
        return s * dec[..., None, None] + st, s

    s_final, prev = lax.scan(step, s0, (jnp.moveaxis(states, 1, 0), jnp.moveaxis(chunk_decay, 1, 0)))
    prev = jnp.moveaxis(prev, 0, 1)
    y_off = jnp.einsum('bcign,bcghpn->bcighp', cc, prev) * jnp.exp(cs)[..., None]
    return (y_diag + y_off).reshape(bsz, t, h, p), s_final


def ssd_stream(u, w_in, conv_w, conv_b, dt_bias, a_log, d_skip, norm_w, w_out, s0_f, s0_b):
    f32 = jnp.float32
    bsz, t, _ = u.shape
    z, xbc, dt = jnp.split(u @ w_in, [SSD_INNER, SSD_INNER + SSD_CONV_DIM], axis=-1)
    xbc = jax.nn.silu(dwconv(xbc, conv_w) + conv_b)
    xs, bm, cm = jnp.split(xbc, [SSD_INNER, SSD_INNER + SSD_GROUPS * SSD_STATE], axis=-1)
    xs = xs.reshape(bsz, t, SSD_HEADS, SSD_HEADDIM)
    bm = bm.reshape(bsz, t, SSD_GROUPS, SSD_STATE)
    cm = cm.reshape(bsz, t, SSD_GROUPS, SSD_STATE)
    dt = jax.nn.softplus(dt.astype(f32).reshape(bsz, t, 2, SSD_HEADS) + dt_bias.astype(f32))
    a = -jnp.exp(a_log.astype(f32))
    y_f, s_f = ssd_scan(xs, dt[:, :, 0], a[0], bm, cm, s0_f)
    flip = lambda v: jnp.flip(v, axis=1)
    y_b, s_b = ssd_scan(flip(xs), flip(dt[:, :, 1]), a[1], flip(bm), flip(cm), s0_b)
    y = y_f + flip(y_b) + d_skip.astype(f32)[:, None] * xs.astype(f32)
    y = y.reshape(bsz, t, SSD_INNER) * jax.nn.silu(z.astype(f32))
    y = y.reshape(bsz, t, SSD_GROUPS, SSD_INNER // SSD_GROUPS)
    y = y * lax.rsqrt(jnp.mean(y * y, axis=-1, keepdims=True) + EPS)
    y = y.reshape(bsz, t, SSD_INNER) * norm_w.astype(f32)
    return y.astype(u.dtype) @ w_out, s_f, s_b


def shortconv_stream(u, w_in, conv_w, w_out, rows):
    bsz, t, d = u.shape
    bg, cg, hv = jnp.split(u @ w_in, 3, axis=-1)
    v = cg * hv
    if rows is None:
        v = dwconv(v, conv_w)
    else:
        v = dwconv(v.reshape(bsz * rows, GRID_W, d), conv_w).reshape(bsz, t, d)
    return (bg * v) @ w_out


def chunk_mlp_stream(u, w_in, v_norm, w_s, b_s, w_out):
    bsz, t, _ = u.shape
    zu, zv = jnp.split(jax.nn.gelu(u @ w_in), 2, axis=-1)
    zv = layernorm(zv, v_norm).reshape(bsz, t // GM_CHUNK, GM_CHUNK, GM_GROUPS, GM_GROUP_DIM)
    s = jnp.einsum('gij,bnjgc->bnigc', w_s, zv) + b_s.T[None, None, :, :, None]
    return (zu * s.reshape(bsz, t, GM_DIM)) @ w_out


def setup_inputs(seed: int = 0) -> dict:
    key = jax.random.key(seed)
    ks = iter(jax.random.split(key, 40))
    f32 = jnp.float32
    d = D_MODEL

    def nrm(shape, scale):
        return jax.random.normal(next(ks), shape, f32) * scale

    x = nrm((BATCH, SEQ, d), 1.0)
    c = nrm((BATCH, d), 1.0)
    ctx = nrm((BATCH, CTX_LEN, d), 1.0)
    c_ctx = nrm((d,), 1.0)
    ada_w = nrm((DEPTH, d, N_MOD * d), 0.5 * d ** -0.5)
    ada_b = nrm((DEPTH, N_MOD * d), 0.02)
    norm_g = 1.0 + nrm((DEPTH, 3, d), 0.02)
    ffn_wg = nrm((DEPTH, 2, d, FFN_DIM), d ** -0.5)
    ffn_wu = nrm((DEPTH, 2, d, FFN_DIM), d ** -0.5)
    ffn_wd = nrm((DEPTH, 2, FFN_DIM, d), FFN_DIM ** -0.5)
    ssd_in = nrm((N_A, d, SSD_PROJ), d ** -0.5)
    ssd_conv_w = nrm((N_A, SSD_CONV, SSD_CONV_DIM), SSD_CONV ** -0.5)
    ssd_conv_b = nrm((N_A, SSD_CONV_DIM), 0.02)
    dt_init = jnp.exp(jax.random.uniform(next(ks), (N_A, 2, SSD_HEADS), f32,
                                         minval=math.log(1e-3), maxval=math.log(1e-1)))
    ssd_dt_bias = dt_init + jnp.log(-jnp.expm1(-dt_init))
    ssd_a_log = jnp.log(jax.random.uniform(next(ks), (N_A, 2, SSD_HEADS), f32, minval=1.0, maxval=16.0))
    ssd_d = 1.0 + nrm((N_A, SSD_HEADS), 0.02)
    ssd_norm = 1.0 + nrm((N_A, SSD_INNER), 0.02)
    ssd_out = nrm((N_A, SSD_INNER, d), SSD_INNER ** -0.5)
    sc_in = nrm((N_B, d, 3 * d), d ** -0.5)
    sc_conv = nrm((N_B, SC_WIDTH, d), SC_WIDTH ** -0.5)
    sc_out = nrm((N_B, d, d), d ** -0.5)
    gm_in = nrm((N_C, d, 2 * GM_DIM), d ** -0.5)
    gm_vnorm = 1.0 + nrm((N_C, GM_DIM), 0.02)
    gm_ws = nrm((N_C, GM_GROUPS, GM_CHUNK, GM_CHUNK), GM_CHUNK ** -0.5)
    gm_bs = nrm((N_C, GM_GROUPS, GM_CHUNK), 0.02)
    gm_out = nrm((N_C, GM_DIM, d), GM_DIM ** -0.5)
    final_norm = 1.0 + nrm((d,), 0.02)
    return {"x": x, "c": c, "ctx": ctx, "c_ctx": c_ctx, "ada_w": ada_w, "ada_b": ada_b,
            "norm_g": norm_g, "ffn_wg": ffn_wg, "ffn_wu": ffn_wu, "ffn_wd": ffn_wd,
            "ssd_in": ssd_in, "ssd_conv_w": ssd_conv_w, "ssd_conv_b": ssd_conv_b,
            "ssd_dt_bias": ssd_dt_bias, "ssd_a_log": ssd_a_log, "ssd_d": ssd_d,
            "ssd_norm": ssd_norm, "ssd_out": ssd_out, "sc_in": sc_in, "sc_conv": sc_conv,
            "sc_out": sc_out, "gm_in": gm_in, "gm_vnorm": gm_vnorm, "gm_ws": gm_ws,
            "gm_bs": gm_bs, "gm_out": gm_out, "final_norm": final_norm}


def reference(x, c, ctx, c_ctx, ada_w, ada_b, norm_g, ffn_wg, ffn_wu, ffn_wd,
              ssd_in, ssd_conv_w, ssd_conv_b, ssd_dt_bias, ssd_a_log, ssd_d, ssd_norm, ssd_out,
              sc_in, sc_conv, sc_out, gm_in, gm_vnorm, gm_ws, gm_bs, gm_out, final_norm):
    rows = x.shape[1] // GRID_W
    bsz = x.shape[0]
    lat, cx = x, ctx
    for i in range(DEPTH):
        last = i == DEPTH - 1
        m_lat = jnp.split((jax.nn.silu(c) @ ada_w[i] + ada_b[i])[:, None, :], N_MOD, axis=-1)
        m_ctx = jnp.split((jax.nn.silu(c_ctx) @ ada_w[i] + ada_b[i])[None, None, :], N_MOD, axis=-1)
        lat = half_ffn(lat, m_lat[0], m_lat[1], m_lat[2], norm_g[i, 0], ffn_wg[i, 0], ffn_wu[i, 0], ffn_wd[i, 0])
        cx = half_ffn(cx, m_ctx[0], m_ctx[1], m_ctx[2], norm_g[i, 0], ffn_wg[i, 0], ffn_wu[i, 0], ffn_wd[i, 0])
        u_lat = modulate(rmsnorm(lat, norm_g[i, 1]), m_lat[3], m_lat[4])
        u_ctx = modulate(rmsnorm(cx, norm_g[i, 1]), m_ctx[3], m_ctx[4])
        kind, j = i % N_MIXERS, i // N_MIXERS
        if kind == 0:
            s0 = jnp.zeros((bsz, SSD_GROUPS, SSD_HPG, SSD_HEADDIM, SSD_STATE), jnp.float32)
            y_ctx, s_f, s_b = ssd_stream(u_ctx, ssd_in[j], ssd_conv_w[j], ssd_conv_b[j], ssd_dt_bias[j],
                                         ssd_a_log[j], ssd_d[j], ssd_norm[j], ssd_out[j], s0, s0)
            y_lat, _, _ = ssd_stream(u_lat, ssd_in[j], ssd_conv_w[j], ssd_conv_b[j], ssd_dt_bias[j],
                                     ssd_a_log[j], ssd_d[j], ssd_norm[j], ssd_out[j], s_f, s_b)
        elif kind == 1:
            y_ctx = shortconv_stream(u_ctx, sc_in[j], sc_conv[j], sc_out[j], None)
            y_lat = shortconv_stream(u_lat, sc_in[j], sc_conv[j], sc_out[j], rows)
        else:
            y_ctx = chunk_mlp_stream(u_ctx, gm_in[j], gm_vnorm[j], gm_ws[j], gm_bs[j], gm_out[j])
            y_lat = chunk_mlp_stream(u_lat, gm_in[j], gm_vnorm[j], gm_ws[j], gm_bs[j], gm_out[j])
        lat = lat + m_lat[5] * y_lat
        lat = half_ffn(lat, m_lat[6], m_lat[7], m_lat[8], norm_g[i, 2], ffn_wg[i, 1], ffn_wu[i, 1], ffn_wd[i, 1])
        if not last:
            cx = cx + m_ctx[5] * y_ctx
            cx = half_ffn(cx, m_ctx[6], m_ctx[7], m_ctx[8], norm_g[i, 2], ffn_wg[i, 1], ffn_wu[i, 1], ffn_wd[i, 1])
    return rmsnorm(lat, final_norm)
```

```python
import functools

import jax
import jax.numpy as jnp
import numpy as np
from jax import lax
from jax.experimental import pallas as pl
from jax.experimental.pallas import tpu as pltpu

F32 = jnp.float32
BF16 = jnp.bfloat16

D_MODEL = 1024
BATCH = 2
SEQ = 8192
DEPTH = 4
GRID_W = 64
CTX_LEN = 256
N_MOD = 9
EPS = 1e-6
FFN_DIM = 2816

SSD_INNER = 2048
SSD_HEADDIM = 64
SSD_HEADS = 32
SSD_STATE = 128
SSD_GROUPS = 4
SSD_HPG = 8
SSD_CONV = 5
SSD_CHUNK = 128
SSD_BC = SSD_GROUPS * SSD_STATE
SSD_CONV_DIM = SSD_INNER + 2 * SSD_BC
SSD_GROUP_W = SSD_HPG * SSD_HEADDIM

GM_CHUNK = 128
GM_DIM = 2048
GM_GROUPS = 8
GM_GROUP_DIM = 256

N_LAT = BATCH * SEQ
N_CTX = BATCH * CTX_LEN
N_TOK = N_LAT + N_CTX
CTX_MOD_ROW = BATCH

LANE = 128
SUBLANE = 8
VMEM_LIMIT = 56 * 1024 * 1024

FFN_TM = 512
FFN_TF = 1408
ROW_TILE = 256
HALO = SUBLANE


def _dot(a, b):
    return jnp.dot(a, b, preferred_element_type=F32)


def _dot_nt(a, b):
    return lax.dot_general(a, b, (((1,), (1,)), ((), ())), preferred_element_type=F32)


def _silu(x):
    return x * jax.nn.sigmoid(x)


def _mod_row(tile, tile_rows):
    return jnp.minimum((tile * tile_rows) // SEQ, CTX_MOD_ROW)


def _norm_mod(h, g, shift, scale):
    y = h * lax.rsqrt(jnp.mean(h * h, axis=-1, keepdims=True) + EPS) * g
    return y * (1.0 + scale) + shift


def _params(sem):
    return pltpu.CompilerParams(dimension_semantics=sem, vmem_limit_bytes=VMEM_LIMIT)


def _mod_kernel(ct_ref, w_ref, b_ref, o_ref, s_ref):
    ct = ct_ref[...]
    s_ref[...] = _silu(ct)
    tn = w_ref.shape[-1]

    def body(k, accs):
        r0 = pl.multiple_of(k * SUBLANE, SUBLANE)
        w8 = w_ref[pl.ds(r0, SUBLANE), :]
        s8 = s_ref[pl.ds(r0, SUBLANE), :]
        return tuple(a + w8 * s8[:, r:r + 1] for r, a in enumerate(accs))

    zero = jnp.zeros((SUBLANE, tn), F32)
    accs = lax.fori_loop(0, D_MODEL // SUBLANE, body, (zero,) * (BATCH + 1), unroll=4)
    o_ref[...] = jnp.zeros(o_ref.shape, F32)
    for r, a in enumerate(accs):
        o_ref[r:r + 1, :] = jnp.sum(a, axis=0, keepdims=True) + b_ref[...]


def _modulations(c, c_ctx, ada_w, ada_b):
    ct = jnp.concatenate([c, c_ctx[None, :], jnp.zeros((SUBLANE - BATCH - 1, D_MODEL), F32)], axis=0).T
    tn = 1024
    n_out = N_MOD * D_MODEL
    out = pl.pallas_call(
        _mod_kernel,
        out_shape=jax.ShapeDtypeStruct((DEPTH, SUBLANE, n_out), F32),
        grid=(DEPTH, n_out // tn),
        in_specs=[
            pl.BlockSpec((D_MODEL, SUBLANE), lambda l, j: (0, 0)),
            pl.BlockSpec((None, D_MODEL, tn), lambda l, j: (l, 0, j)),
            pl.BlockSpec((None, 1, tn), lambda l, j: (l, 0, j)),
        ],
        out_specs=pl.BlockSpec((None, SUBLANE, tn), lambda l, j: (l, 0, j)),
        scratch_shapes=[pltpu.VMEM((D_MODEL, SUBLANE), F32)],
        compiler_params=_params(("parallel", "parallel")),
        name="adaln_mod",
    )(ct, ada_w, ada_b.reshape(DEPTH, 1, n_out))
    return out.reshape(DEPTH, SUBLANE, N_MOD, D_MODEL)


def _ffn_kernel(h_ref, mod_ref, g_ref, wg_ref, wu_ref, wd_ref, *rest, mod_base, final):
    if final:
        fg_ref, o_ref, u_ref, acc_ref = rest
    else:
        o_ref, u_ref, acc_ref = rest
    j = pl.program_id(1)

    @pl.when(j == 0)
    def _():
        u = _norm_mod(h_ref[...], g_ref[...], mod_ref[mod_base:mod_base + 1, :],
                      mod_ref[mod_base + 1:mod_base + 2, :])
        u_ref[...] = u.astype(BF16)
        acc_ref[...] = jnp.zeros(acc_ref.shape, F32)

    u = u_ref[...]
    a = _silu(_dot(u, wg_ref[...])) * _dot(u, wu_ref[...])
    acc_ref[...] += _dot(a.astype(BF16), wd_ref[...])

    @pl.when(j == pl.num_programs(1) - 1)
    def _():
        out = h_ref[...] + 0.5 * mod_ref[mod_base + 2:mod_base + 3, :] * acc_ref[...]
        if final:
            out = out * lax.rsqrt(jnp.mean(out * out, axis=-1, keepdims=True) + EPS) * fg_ref[...]
        o_ref[...] = out


def _half_ffn(h, mods_l, g, wg, wu, wd, mod_base, n_rows, final_g=None):
    final = final_g is not None
    nf = FFN_DIM // FFN_TF
    row = lambda i, j: (i, 0)
    in_specs = [
        pl.BlockSpec((FFN_TM, D_MODEL), row),
        pl.BlockSpec((None, N_MOD, D_MODEL), lambda i, j: (_mod_row(i, FFN_TM), 0, 0)),
        pl.BlockSpec((1, D_MODEL), lambda i, j: (0, 0)),
        pl.BlockSpec((D_MODEL, FFN_TF), lambda i, j: (0, j)),
        pl.BlockSpec((D_MODEL, FFN_TF), lambda i, j: (0, j)),
        pl.BlockSpec((FFN_TF, D_MODEL), lambda i, j: (j, 0)),
    ]
    args = [h, mods_l, g.reshape(1, D_MODEL), wg, wu, wd]
    if final:
        in_specs.append(pl.BlockSpec((1, D_MODEL), lambda i, j: (0, 0)))
        args.append(final_g.reshape(1, D_MODEL))
    return pl.pallas_call(
        functools.partial(_ffn_kernel, mod_base=mod_base, final=final),
        out_shape=jax.ShapeDtypeStruct((n_rows, D_MODEL), F32),
        grid=(n_rows // FFN_TM, nf),
        in_specs=in_specs,
        out_specs=pl.BlockSpec((FFN_TM, D_MODEL), row),
        scratch_shapes=[pltpu.VMEM((FFN_TM, D_MODEL), BF16), pltpu.VMEM((FFN_TM, D_MODEL), F32)],
        compiler_params=_params(("parallel", "arbitrary")),
        name="half_ffn_final" if final else "half_ffn",
    )(*args)


def _in_proj_kernel(h_ref, mod_ref, g_ref, *rest, n_out):
    w_refs, o_refs = rest[:n_out], rest[n_out:]
    u = _norm_mod(h_ref[...], g_ref[...], mod_ref[3:4, :], mod_ref[4:5, :]).astype(BF16)
    for w_ref, o_ref in zip(w_refs, o_refs):
        o_ref[...] = _dot(u, w_ref[...]).astype(o_ref.dtype)


def _in_proj(h, mods_l, g, weights, name):
    row = lambda i: (i, 0)
    const = lambda i: (0, 0)
    return pl.pallas_call(
        functools.partial(_in_proj_kernel, n_out=len(weights)),
        out_shape=[jax.ShapeDtypeStruct((N_TOK, w.shape[1]), F32) for w in weights],
        grid=(N_TOK // ROW_TILE,),
        in_specs=[
            pl.BlockSpec((ROW_TILE, D_MODEL), row),
            pl.BlockSpec((None, N_MOD, D_MODEL), lambda i: (_mod_row(i, ROW_TILE), 0, 0)),
            pl.BlockSpec((1, D_MODEL), const),
        ] + [pl.BlockSpec(w.shape, const) for w in weights],
        out_specs=[pl.BlockSpec((ROW_TILE, w.shape[1]), row) for w in weights],
        compiler_params=_params(("parallel",)),
        name=name,
    )(h, mods_l, g.reshape(1, D_MODEL), *weights)


def _out_proj_kernel(h_ref, mod_ref, a_ref, w_ref, o_ref):
    o_ref[...] = h_ref[...] + mod_ref[5:6, :] * _dot(a_ref[...], w_ref[...])


def _out_proj(h, mods_l, a, w, name):
    tm = 512
    row = lambda i: (i, 0)
    return pl.pallas_call(
        _out_proj_kernel,
        out_shape=jax.ShapeDtypeStruct((N_TOK, D_MODEL), F32),
        grid=(N_TOK // tm,),
        in_specs=[
            pl.BlockSpec((tm, D_MODEL), row),
            pl.BlockSpec((None, N_MOD, D_MODEL), lambda i: (_mod_row(i, tm), 0, 0)),
            pl.BlockSpec((tm, a.shape[1]), row),
            pl.BlockSpec(w.shape, lambda i: (0, 0)),
        ],
        out_specs=pl.BlockSpec((tm, D_MODEL), row),
        compiler_params=_params(("parallel",)),
        name=name,
    )(h, mods_l, a, w)


def _seq_edges(tile):
    lat_tiles = SEQ // ROW_TILE
    is_ctx = tile >= BATCH * lat_tiles
    pos = tile % lat_tiles
    return jnp.logical_or(is_ctx, pos == 0), jnp.logical_or(is_ctx, pos == lat_tiles - 1)


def _ssd_conv_kernel(prev_ref, cur_ref, next_ref, w_ref, b_ref, o_ref, ext_ref):
    first, last = _seq_edges(pl.program_id(0))
    ext_ref[0:HALO, :] = jnp.where(first, 0.0, prev_ref[...])
    ext_ref[HALO:HALO + ROW_TILE, :] = cur_ref[...]
    ext_ref[HALO + ROW_TILE:, :] = jnp.where(last, 0.0, next_ref[...])
    half = SSD_CONV // 2
    acc = b_ref[...] + w_ref[half:half + 1, :] * cur_ref[...]
    for k in range(SSD_CONV):
        if k != half:
            lo = HALO + k - half
            acc = acc + w_ref[k:k + 1, :] * ext_ref[lo:lo + ROW_TILE, :]
    o_ref[...] = _silu(acc)


def _ssd_conv(xbc, conv_w, conv_b):
    n_tiles = N_TOK // ROW_TILE
    per = ROW_TILE // HALO
    n_halo = N_TOK // HALO
    c = SSD_CONV_DIM
    return pl.pallas_call(
        _ssd_conv_kernel,
        out_shape=jax.ShapeDtypeStruct((N_TOK, c), F32),
        grid=(n_tiles,),
        in_specs=[
            pl.BlockSpec((HALO, c), lambda i: (jnp.maximum(i * per - 1, 0), 0)),
            pl.BlockSpec((ROW_TILE, c), lambda i: (i, 0)),
            pl.BlockSpec((HALO, c), lambda i: (jnp.minimum((i + 1) * per, n_halo - 1), 0)),
            pl.BlockSpec((SSD_CONV, c), lambda i: (0, 0)),
            pl.BlockSpec((1, c), lambda i: (0, 0)),
        ],
        out_specs=pl.BlockSpec((ROW_TILE, c), lambda i: (i, 0)),
        scratch_shapes=[pltpu.VMEM((ROW_TILE + 2 * HALO, c), F32)],
        compiler_params=_params(("parallel",)),
        name="ssd_conv",
    )(xbc, xbc, xbc, conv_w, conv_b.reshape(1, c))


def _split_hi_lo(x):
    hi = x.astype(BF16)
    lo = (x - hi.astype(F32)).astype(BF16)
    return jnp.concatenate([hi, lo], axis=1)


def _ssd_scan_kernel(xbc_ref, dt_ref, bias_ref, alog_ref, e2_ref, *rest, reverse):
    if reverse:
        yf_ref, z_ref, dexp_ref, nw_ref, o_ref, st_ref, y_ref = rest
    else:
        o_ref, st_ref = rest
        y_ref = o_ref
    q = SSD_CHUNK
    off = SSD_HEADS if reverse else 0

    @pl.when(pl.program_id(1) == 0)
    def _():
        st_ref[...] = jnp.zeros(st_ref.shape, F32)

    x_raw = dt_ref[...] + bias_ref[...]
    dt = jnp.maximum(x_raw, 0.0) + jnp.log1p(jnp.exp(-jnp.abs(x_raw)))
    da = dt * (-jnp.exp(alog_ref[...]))
    ii = lax.broadcasted_iota(jnp.int32, (q, q), 0)
    jj = lax.broadcasted_iota(jnp.int32, (q, q), 1)
    tri = (jj >= ii) if reverse else (jj <= ii)
    cs = jnp.dot(tri.astype(F32), da, precision=lax.Precision.HIGHEST, preferred_element_type=F32)
    total = cs[0:1, :] if reverse else cs[q - 1:q, :]
    cs_t = cs.T
    dt_t = dt.T
    ecs = jnp.exp(cs)
    w_end = dt * jnp.exp(total - cs)
    chunk_decay = jnp.broadcast_to(jnp.exp(total), (2 * SUBLANE, LANE))
    spread = _dot(_split_hi_lo(jnp.concatenate([w_end, chunk_decay], axis=0)), e2_ref[...])
    w_exp = spread[0:q, :]
    decay_exp = spread[q:q + 1, :]

    lane = lax.broadcasted_iota(jnp.int32, (q, LANE), 1)
    for g in range(SSD_GROUPS):
        b_g = xbc_ref[:, SSD_INNER + g * SSD_STATE:SSD_INNER + (g + 1) * SSD_STATE]
        c_g = xbc_ref[:, SSD_INNER + SSD_BC + g * SSD_STATE:SSD_INNER + SSD_BC + (g + 1) * SSD_STATE]
        cb = _dot_nt(c_g.astype(BF16), b_g.astype(BF16))
        gs = slice(g * SSD_GROUP_W, (g + 1) * SSD_GROUP_W)
        st_old = st_ref[:, gs]
        st_bf = st_old.astype(BF16)
        x_g = xbc_ref[:, gs]
        for k in range(SSD_GROUP_W // LANE):
            ls = slice(k * LANE, (k + 1) * LANE)
            rhs = jnp.concatenate([x_g[:, ls].astype(BF16), st_bf[:, ls]], axis=0)
            outs = []
            for h in (g * SSD_HPG + 2 * k, g * SSD_HPG + 2 * k + 1):
                col = off + h
                seg = cs[:, col:col + 1] - cs_t[col:col + 1, :]
                m = jnp.exp(jnp.where(tri, seg, -jnp.inf)) * cb * dt_t[col:col + 1, :]
                c_scaled = c_g * ecs[:, col:col + 1]
                lhs = jnp.concatenate([m.astype(BF16), c_scaled.astype(BF16)], axis=1)
                outs.append(_dot(lhs, rhs))
            y_pair = jnp.where(lane < SSD_HEADDIM, outs[0], outs[1])
            cols = slice(g * SSD_GROUP_W + k * LANE, g * SSD_GROUP_W + (k + 1) * LANE)
            if reverse:
                y_pair = y_pair + yf_ref[:, cols] + dexp_ref[:, cols] * x_g[:, ls]
                zz = z_ref[:, cols]
                y_pair = y_pair * _silu(zz)
            y_ref[:, cols] = y_pair
        xw = (x_g * w_exp[:, gs]).astype(BF16)
        st_ref[:, gs] = st_old * decay_exp[:, gs] + _dot(b_g.T.astype(BF16), xw)

    if reverse:
        for g in range(SSD_GROUPS):
            gs = slice(g * SSD_GROUP_W, (g + 1) * SSD_GROUP_W)
            y = y_ref[:, gs]
            y = y * lax.rsqrt(jnp.mean(y * y, axis=-1, keepdims=True) + EPS) * nw_ref[:, gs]
            o_ref[:, gs] = y.astype(o_ref.dtype)


def _ssd_scan(xbc_act, dt_raw, bias_pad, alog_pad, e2, reverse, extra=()):
    ctx_chunks = CTX_LEN // SSD_CHUNK
    lat_chunks = SEQ // SSD_CHUNK
    ctx_base = N_LAT // SSD_CHUNK
    steps = ctx_chunks + lat_chunks

    def chunk(b, s):
        if reverse:
            c_ctx, c_lat = ctx_chunks - 1 - s, steps - 1 - s
        else:
            c_ctx, c_lat = s, s - ctx_chunks
        return jnp.where(s < ctx_chunks, ctx_base + b * ctx_chunks + c_ctx, b * lat_chunks + c_lat)

    rows = lambda w: pl.BlockSpec((SSD_CHUNK, w), lambda b, s: (chunk(b, s), 0))
    const = lambda shape: pl.BlockSpec(shape, lambda b, s: (0, 0))
    in_specs = [rows(SSD_CONV_DIM), rows(LANE), const((1, LANE)), const((1, LANE)), const(e2.shape)]
    args = [xbc_act, dt_raw, bias_pad, alog_pad, e2]
    if reverse:
        y_f, z, d_exp, norm_w = extra
        in_specs += [rows(SSD_INNER), rows(SSD_INNER), const((1, SSD_INNER)), const((1, SSD_INNER))]
        args += [y_f, z, d_exp, norm_w]
    scratch = [pltpu.VMEM((SSD_STATE, SSD_INNER), F32)]
    if reverse:
        scratch.append(pltpu.VMEM((SSD_CHUNK, SSD_INNER), F32))
    return pl.pallas_call(
        functools.partial(_ssd_scan_kernel, reverse=reverse),
        out_shape=jax.ShapeDtypeStruct((N_TOK, SSD_INNER), BF16 if reverse else F32),
        grid=(BATCH, steps),
        in_specs=in_specs,
        out_specs=rows(SSD_INNER),
        scratch_shapes=scratch,
        compiler_params=_params(("parallel", "arbitrary")),
        name="ssd_scan_bwd" if reverse else "ssd_scan_fwd",
    )(*args)


def _head_spread_matrix(off):
    e = np.zeros((LANE, SSD_INNER), np.float32)
    for h in range(SSD_HEADS):
        e[off + h, h * SSD_HEADDIM:(h + 1) * SSD_HEADDIM] = 1.0
    return jnp.asarray(np.concatenate([e, e], axis=0), BF16)


def _ssd_mixer(h, mods_l, g, w_in, conv_w, conv_b, dt_bias, a_log, d_skip, norm_w, w_out):
    wz = w_in[:, :SSD_INNER].astype(BF16)
    wxbc = w_in[:, SSD_INNER:SSD_INNER + SSD_CONV_DIM].astype(BF16)
    n_dt = 2 * SSD_HEADS
    wdt = jnp.pad(w_in[:, SSD_INNER + SSD_CONV_DIM:], ((0, 0), (0, LANE - n_dt))).astype(BF16)
    z, xbc, dt_raw = _in_proj(h, mods_l, g, [wz, wxbc, wdt], "ssd_in_proj")
    xbc_act = _ssd_conv(xbc, conv_w, conv_b)
    bias_pad = jnp.pad(dt_bias.reshape(1, n_dt), ((0, 0), (0, LANE - n_dt)))
    alog_pad = jnp.pad(a_log.reshape(1, n_dt), ((0, 0), (0, LANE - n_dt)))
    y_f = _ssd_scan(xbc_act, dt_raw, bias_pad, alog_pad, _head_spread_matrix(0), False)
    d_exp = jnp.repeat(d_skip, SSD_HEADDIM).reshape(1, SSD_INNER)
    y = _ssd_scan(xbc_act, dt_raw, bias_pad, alog_pad, _head_spread_matrix(SSD_HEADS), True,
                  extra=(y_f, z, d_exp, norm_w.reshape(1, SSD_INNER)))
    return _out_proj(h, mods_l, y, w_out.astype(BF16), "ssd_out_proj")


def _shortconv_kernel(h_ref, mod_ref, g_ref, wb_ref, wc_ref, wh_ref, cw_ref, wo_ref, o_ref, v_ref):
    h = h_ref[...]
    u = _norm_mod(h, g_ref[...], mod_ref[3:4, :], mod_ref[4:5, :]).astype(BF16)
    v = _dot(u, wc_ref[...]) * _dot(u, wh_ref[...])
    v_ref[0:HALO, :] = jnp.zeros((HALO, D_MODEL), F32)
    v_ref[HALO:HALO + ROW_TILE, :] = v
    v_ref[HALO + ROW_TILE:, :] = jnp.zeros((HALO, D_MODEL), F32)
    is_ctx = pl.program_id(0) >= N_LAT // ROW_TILE
    row_len = jnp.where(is_ctx, CTX_LEN, GRID_W)
    pos = lax.broadcasted_iota(jnp.int32, (ROW_TILE, 1), 0) & (row_len - 1)
    v_prev = jnp.where(pos == 0, 0.0, v_ref[HALO - 1:HALO - 1 + ROW_TILE, :])
    v_next = jnp.where(pos == row_len - 1, 0.0, v_ref[HALO + 1:HALO + 1 + ROW_TILE, :])
    conv = cw_ref[0:1, :] * v_prev + cw_ref[1:2, :] * v + cw_ref[2:3, :] * v_next
    y = _dot((_dot(u, wb_ref[...]) * conv).astype(BF16), wo_ref[...])
    o_ref[...] = h + mod_ref[5:6, :] * y


def _shortconv_mixer(h, mods_l, g, w_in, conv_w, w_out):
    wb, wc, wh = (w_in[:, k * D_MODEL:(k + 1) * D_MODEL].astype(BF16) for k in range(3))
    row = lambda i: (i, 0)
    const = lambda i: (0, 0)
    wspec = pl.BlockSpec((D_MODEL, D_MODEL), const)
    return pl.pallas_call(
        _shortconv_kernel,
        out_shape=jax.ShapeDtypeStruct((N_TOK, D_MODEL), F32),
        grid=(N_TOK // ROW_TILE,),
        in_specs=[
            pl.BlockSpec((ROW_TILE, D_MODEL), row),
            pl.BlockSpec((None, N_MOD, D_MODEL), lambda i: (_mod_row(i, ROW_TILE), 0, 0)),
            pl.BlockSpec((1, D_MODEL), const),
            wspec, wspec, wspec,
            pl.BlockSpec(conv_w.shape, const),
            wspec,
        ],
        out_specs=pl.BlockSpec((ROW_TILE, D_MODEL), row),
        scratch_shapes=[pltpu.VMEM((ROW_TILE + 2 * HALO, D_MODEL), F32)],
        compiler_params=_params(("parallel",)),
        name="shortconv_mixer",
    )(h, mods_l, g.reshape(1, D_MODEL), wb, wc, wh, conv_w, w_out.astype(BF16))


def _gmlp_kernel(h_ref, mod_ref, g_ref, wu_ref, wv_ref, vn_ref, ws_ref, bs_ref, wo_ref, o_ref, a_ref):
    h = h_ref[...]
    u = _norm_mod(h, g_ref[...], mod_ref[3:4, :], mod_ref[4:5, :]).astype(BF16)
    zu = jax.nn.gelu(_dot(u, wu_ref[...]), approximate=True)
    zv = jax.nn.gelu(_dot(u, wv_ref[...]), approximate=True)
    zc = zv - jnp.mean(zv, axis=-1, keepdims=True)
    zv = zc * lax.rsqrt(jnp.mean(zc * zc, axis=-1, keepdims=True) + EPS) * vn_ref[...]
    zv = zv.astype(BF16)
    for c in range(ROW_TILE // GM_CHUNK):
        rs = slice(c * GM_CHUNK, (c + 1) * GM_CHUNK)
        for g in range(GM_GROUPS):
            gs = slice(g * GM_GROUP_DIM, (g + 1) * GM_GROUP_DIM)
            s = _dot(ws_ref[g], zv[rs, gs]) + bs_ref[:, g:g + 1]
            a_ref[rs, gs] = (zu[rs, gs] * s).astype(BF16)
    o_ref[...] = h + mod_ref[5:6, :] * _dot(a_ref[...], wo_ref[...])


def _gmlp_mixer(h, mods_l, g, w_in, v_norm, w_s, b_s, w_out):
    wu = w_in[:, :GM_DIM].astype(BF16)
    wv = w_in[:, GM_DIM:].astype(BF16)
    row = lambda i: (i, 0)
    const = lambda i: (0, 0)
    return pl.pallas_call(
        _gmlp_kernel,
        out_shape=jax.ShapeDtypeStruct((N_TOK, D_MODEL), F32),
        grid=(N_TOK // ROW_TILE,),
        in_specs=[
            pl.BlockSpec((ROW_TILE, D_MODEL), row),
            pl.BlockSpec((None, N_MOD, D_MODEL), lambda i: (_mod_row(i, ROW_TILE), 0, 0)),
            pl.BlockSpec((1, D_MODEL), const),
            pl.BlockSpec((D_MODEL, GM_DIM), const),
            pl.BlockSpec((D_MODEL, GM_DIM), const),
            pl.BlockSpec((1, GM_DIM), const),
            pl.BlockSpec(w_s.shape, lambda i: (0, 0, 0)),
            pl.BlockSpec((GM_CHUNK, GM_GROUPS), const),
            pl.BlockSpec((GM_DIM, D_MODEL), const),
        ],
        out_specs=pl.BlockSpec((ROW_TILE, D_MODEL), row),
        scratch_shapes=[pltpu.VMEM((ROW_TILE, GM_DIM), BF16)],
        compiler_params=_params(("parallel",)),
        name="gmlp_mixer",
    )(h, mods_l, g.reshape(1, D_MODEL), wu, wv, v_norm.reshape(1, GM_DIM), w_s.astype(BF16),
      b_s.T, w_out.astype(BF16))


def kernel(x, c, ctx, c_ctx, ada_w, ada_b, norm_g, ffn_wg, ffn_wu, ffn_wd, ssd_in, ssd_conv_w, ssd_conv_b, ssd_dt_bias, ssd_a_log, ssd_d, ssd_norm, ssd_out, sc_in, sc_conv, sc_out, gm_in, gm_vnorm, gm_ws, gm_bs, gm_out, final_norm):
    h = jnp.concatenate([x.reshape(N_LAT, D_MODEL), ctx.reshape(N_CTX, D_MODEL)], axis=0)
    mods = _modulations(c, c_ctx, ada_w, ada_b)
    for i in range(DEPTH):
        last = i == DEPTH - 1
        m = mods[i]
        h = _half_ffn(h, m, norm_g[i, 0], ffn_wg[i, 0].astype(BF16), ffn_wu[i, 0].astype(BF16),
                      ffn_wd[i, 0].astype(BF16), 0, N_TOK)
        kind, j = i % 3, i // 3
        if kind == 0:
            h = _ssd_mixer(h, m, norm_g[i, 1], ssd_in[j], ssd_conv_w[j], ssd_conv_b[j], ssd_dt_bias[j],
                           ssd_a_log[j], ssd_d[j], ssd_norm[j], ssd_out[j])
        elif kind == 1:
            h = _shortconv_mixer(h, m, norm_g[i, 1], sc_in[j], sc_conv[j], sc_out[j])
        else:
            h = _gmlp_mixer(h, m, norm_g[i, 1], gm_in[j], gm_vnorm[j], gm_ws[j], gm_bs[j], gm_out[j])
        h = _half_ffn(h, m, norm_g[i, 2], ffn_wg[i, 1].astype(BF16), ffn_wu[i, 1].astype(BF16),
                      ffn_wd[i, 1].astype(BF16), 6, N_LAT if last else N_TOK,
                      final_g=final_norm if last else None)
    return h.reshape(BATCH, SEQ, D_MODEL)
```

```python
import functools

import jax
import jax.numpy as jnp
import numpy as np
from jax import lax
from jax.experimental import pallas as pl
from jax.experimental.pallas import tpu as pltpu

F32 = jnp.float32
BF16 = jnp.bfloat16

D_MODEL = 1024
BATCH = 2
SEQ = 8192
DEPTH = 4
GRID_W = 64
CTX_LEN = 256
N_MOD = 9
EPS = 1e-6
LOG2E = 1.4426950408889634
FFN_DIM = 2816

SSD_INNER = 2048
SSD_HEADDIM = 64
SSD_HEADS = 32
SSD_STATE = 128
SSD_GROUPS = 4
SSD_HPG = 8
SSD_CONV = 5
SSD_CHUNK = 128
SSD_BC = SSD_GROUPS * SSD_STATE
SSD_CONV_DIM = SSD_INNER + 2 * SSD_BC
SSD_GROUP_W = SSD_HPG * SSD_HEADDIM

GM_CHUNK = 128
GM_DIM = 2048
GM_GROUPS = 8
GM_GROUP_DIM = 256

N_LAT = BATCH * SEQ
N_CTX = BATCH * CTX_LEN
N_TOK = N_LAT + N_CTX
CTX_MOD_ROW = BATCH

LANE = 128
SUBLANE = 8
VMEM_LIMIT = 56 * 1024 * 1024

FFN_TM = 512
FFN_FC = 256
ROW_TILE = 256
HALO = SUBLANE
CONV_COLS = 256


def _dot(a, b):
    return jnp.dot(a, b, preferred_element_type=F32)


def _dot_nt(a, b):
    return lax.dot_general(a, b, (((1,), (1,)), ((), ())), preferred_element_type=F32)


def _silu(x):
    hx = 0.5 * x
    return hx + hx * jnp.tanh(hx)


def _mod_row(tile, tile_rows):
    return jnp.minimum((tile * tile_rows) // SEQ, CTX_MOD_ROW)


def _norm_mod(h, g, shift, scale):
    y = h * lax.rsqrt(jnp.mean(h * h, axis=-1, keepdims=True) + EPS) * g
    return y * (1.0 + scale) + shift


def _params(sem):
    return pltpu.CompilerParams(dimension_semantics=sem, vmem_limit_bytes=VMEM_LIMIT)


def _mod_kernel(ct_ref, w_ref, b_ref, o_ref, s_ref):
    n_vec = BATCH + 1
    tn = w_ref.shape[-1]

    @pl.when(jnp.logical_and(pl.program_id(0) == 0, pl.program_id(1) == 0))
    def _():
        s = _silu(ct_ref[...])
        for r in range(n_vec):
            s_ref[r] = jnp.broadcast_to(s[:, r:r + 1], (D_MODEL, LANE))

    def body(k, accs):
        r0 = pl.multiple_of(k * SUBLANE, SUBLANE)
        w8 = w_ref[pl.ds(r0, SUBLANE), :]
        out = []
        for r, a in enumerate(accs):
            s8 = s_ref[r, pl.ds(r0, SUBLANE), :]
            out.append(a + w8 * jnp.concatenate([s8] * (tn // LANE), axis=1))
        return tuple(out)

    zero = jnp.zeros((SUBLANE, tn), F32)
    accs = lax.fori_loop(0, D_MODEL // SUBLANE, body, (zero,) * n_vec, unroll=4)
    o_ref[...] = jnp.zeros(o_ref.shape, F32)
    for r, a in enumerate(accs):
        o_ref[r:r + 1, :] = jnp.sum(a, axis=0, keepdims=True) + b_ref[...]


def _modulations(c, c_ctx, ada_w, ada_b):
    ct = jnp.concatenate([c, c_ctx[None, :], jnp.zeros((SUBLANE - BATCH - 1, D_MODEL), F32)], axis=0).T
    tn = 1024
    n_out = N_MOD * D_MODEL
    out = pl.pallas_call(
        _mod_kernel,
        out_shape=jax.ShapeDtypeStruct((DEPTH, SUBLANE, n_out), F32),
        grid=(DEPTH, n_out // tn),
        in_specs=[
            pl.BlockSpec((D_MODEL, SUBLANE), lambda l, j: (0, 0)),
            pl.BlockSpec((None, D_MODEL, tn), lambda l, j: (l, 0, j)),
            pl.BlockSpec((None, 1, tn), lambda l, j: (l, 0, j)),
        ],
        out_specs=pl.BlockSpec((None, SUBLANE, tn), lambda l, j: (l, 0, j)),
        scratch_shapes=[pltpu.VMEM((BATCH + 1, D_MODEL, LANE), F32)],
        compiler_params=_params(("arbitrary", "arbitrary")),
        name="adaln_mod",
    )(ct, ada_w, ada_b.reshape(DEPTH, 1, n_out))
    return out.reshape(DEPTH, SUBLANE, N_MOD, D_MODEL)


def _ffn_kernel(h_ref, mod_ref, g_ref, wg_ref, wu_ref, wd_ref, *rest, mod_base, final):
    if final:
        fg_ref, o_ref, a_ref = rest
    else:
        o_ref, a_ref = rest
    h = h_ref[...]
    u = _norm_mod(h, g_ref[...], mod_ref[mod_base:mod_base + 1, :],
                  mod_ref[mod_base + 1:mod_base + 2, :]).astype(BF16)
    for c0 in range(0, FFN_DIM, FFN_FC):
        cs = slice(c0, c0 + FFN_FC)
        a_ref[:, cs] = (_silu(_dot(u, wg_ref[:, cs])) * _dot(u, wu_ref[:, cs])).astype(BF16)
    out = h + 0.5 * mod_ref[mod_base + 2:mod_base + 3, :] * _dot(a_ref[...], wd_ref[...])
    if final:
        out = out * lax.rsqrt(jnp.mean(out * out, axis=-1, keepdims=True) + EPS) * fg_ref[...]
    o_ref[...] = out


def _resident(shape):
    return pl.BlockSpec(shape, lambda *_: (0,) * len(shape), pipeline_mode=pl.Buffered(1))


def _half_ffn(h, mods_l, g, wg, wu, wd, mod_base, n_rows, final_g=None):
    final = final_g is not None
    row = lambda i: (i, 0)
    in_specs = [
        pl.BlockSpec((FFN_TM, D_MODEL), row),
        pl.BlockSpec((None, N_MOD, D_MODEL), lambda i: (_mod_row(i, FFN_TM), 0, 0)),
        _resident((1, D_MODEL)),
        _resident(wg.shape), _resident(wu.shape), _resident(wd.shape),
    ]
    args = [h, mods_l, g.reshape(1, D_MODEL), wg, wu, wd]
    if final:
        in_specs.append(_resident((1, D_MODEL)))
        args.append(final_g.reshape(1, D_MODEL))
    return pl.pallas_call(
        functools.partial(_ffn_kernel, mod_base=mod_base, final=final),
        out_shape=jax.ShapeDtypeStruct((n_rows, D_MODEL), F32),
        grid=(n_rows // FFN_TM,),
        in_specs=in_specs,
        out_specs=pl.BlockSpec((FFN_TM, D_MODEL), row),
        scratch_shapes=[pltpu.VMEM((FFN_TM, FFN_DIM), BF16)],
        compiler_params=_params(("parallel",)),
        name="half_ffn_final" if final else "half_ffn",
    )(*args)


def _out_proj_kernel(h_ref, mod_ref, a_ref, w_ref, o_ref):
    o_ref[...] = h_ref[...] + mod_ref[5:6, :] * _dot(a_ref[...], w_ref[...])


def _out_proj(h, mods_l, a, w, name):
    tm = 512
    row = lambda i: (i, 0)
    return pl.pallas_call(
        _out_proj_kernel,
        out_shape=jax.ShapeDtypeStruct((N_TOK, D_MODEL), F32),
        grid=(N_TOK // tm,),
        in_specs=[
            pl.BlockSpec((tm, D_MODEL), row),
            pl.BlockSpec((None, N_MOD, D_MODEL), lambda i: (_mod_row(i, tm), 0, 0)),
            pl.BlockSpec((tm, a.shape[1]), row),
            pl.BlockSpec(w.shape, lambda i: (0, 0)),
        ],
        out_specs=pl.BlockSpec((tm, D_MODEL), row),
        compiler_params=_params(("parallel",)),
        name=name,
    )(h, mods_l, a, w)


def _seq_edges(tile):
    lat_tiles = SEQ // ROW_TILE
    is_ctx = tile >= BATCH * lat_tiles
    pos = tile % lat_tiles
    return jnp.logical_or(is_ctx, pos == 0), jnp.logical_or(is_ctx, pos == lat_tiles - 1)


def _ssd_in_kernel(hp_ref, h_ref, hn_ref, mod_ref, g_ref, wz_ref, wx_ref, wdt_ref, cw_ref, cb_ref,
                   z_ref, x_ref, dt_ref):
    first, last = _seq_edges(pl.program_id(0))
    g, shift, scale = g_ref[...], mod_ref[3:4, :], mod_ref[4:5, :]
    u = _norm_mod(h_ref[...], g, shift, scale)
    u_ext = jnp.concatenate([_norm_mod(hp_ref[...], g, shift, scale), u,
                             _norm_mod(hn_ref[...], g, shift, scale)], axis=0).astype(BF16)
    ub = u.astype(BF16)
    n_grp = (ROW_TILE + 2 * HALO) // SUBLANE
    half = SSD_CONV // 2
    sub = lax.broadcasted_iota(jnp.int32, (1, SUBLANE, 1), 1)

    def project(blk):
        cols = slice(blk * CONV_COLS, (blk + 1) * CONV_COLS)
        e = _dot(u_ext, wx_ref[:, cols])
        e = jnp.concatenate([jnp.where(first, 0.0, e[0:HALO, :]), e[HALO:HALO + ROW_TILE, :],
                             jnp.where(last, 0.0, e[HALO + ROW_TILE:, :])], axis=0)
        return e.reshape(n_grp, SUBLANE, CONV_COLS)

    def conv(blk, e3):
        cols = slice(blk * CONV_COLS, (blk + 1) * CONV_COLS)
        acc = cb_ref[:, cols] + cw_ref[half:half + 1, cols] * e3[1:n_grp - 1]
        for k in range(SSD_CONV):
            s = k - half
            if s == 0:
                continue
            r3 = pltpu.roll(e3, (-s) % SUBLANE, axis=1)
            if s > 0:
                shifted = jnp.where(sub < SUBLANE - s, r3[1:n_grp - 1], r3[2:n_grp])
            else:
                shifted = jnp.where(sub >= -s, r3[1:n_grp - 1], r3[0:n_grp - 2])
            acc = acc + cw_ref[k:k + 1, cols] * shifted
        x_ref[:, cols] = _silu(acc).reshape(ROW_TILE, CONV_COLS)

    n_x = SSD_CONV_DIM // CONV_COLS
    n_z = SSD_INNER // CONV_COLS
    e3 = project(0)
    for blk in range(n_x):
        e3_next = project(blk + 1) if blk + 1 < n_x else None
        if blk < n_z:
            cols = slice(blk * CONV_COLS, (blk + 1) * CONV_COLS)
            z_ref[:, cols] = _dot(ub, wz_ref[:, cols])
        elif blk == n_z:
            dt_ref[...] = _dot(ub, wdt_ref[...])
        conv(blk, e3)
        e3 = e3_next


def _ssd_in(h, mods_l, g, wz, wx, wdt, conv_w, conv_b):
    per = ROW_TILE // HALO
    n_halo = N_TOK // HALO
    row = lambda i: (i, 0)
    return pl.pallas_call(
        _ssd_in_kernel,
        out_shape=[jax.ShapeDtypeStruct((N_TOK, w.shape[1]), F32) for w in (wz, wx, wdt)],
        grid=(N_TOK // ROW_TILE,),
        in_specs=[
            pl.BlockSpec((HALO, D_MODEL), lambda i: (jnp.maximum(i * per - 1, 0), 0)),
            pl.BlockSpec((ROW_TILE, D_MODEL), row),
            pl.BlockSpec((HALO, D_MODEL), lambda i: (jnp.minimum((i + 1) * per, n_halo - 1), 0)),
            pl.BlockSpec((None, N_MOD, D_MODEL), lambda i: (_mod_row(i, ROW_TILE), 0, 0)),
            _resident((1, D_MODEL)),
            _resident(wz.shape), _resident(wx.shape), _resident(wdt.shape),
            _resident(conv_w.shape), _resident((1, SSD_CONV_DIM)),
        ],
        out_specs=[pl.BlockSpec((ROW_TILE, w.shape[1]), row) for w in (wz, wx, wdt)],
        compiler_params=_params(("parallel",)),
        name="ssd_in_proj",
    )(h, h, h, mods_l, g.reshape(1, D_MODEL), wz, wx, wdt, conv_w, conv_b.reshape(1, SSD_CONV_DIM))


def _split_hi_lo(x):
    hi = x.astype(BF16)
    lo = (x - hi.astype(F32)).astype(BF16)
    return jnp.concatenate([hi, lo], axis=1)


def _ssd_scan_kernel(xbc_ref, dt_ref, bias_ref, alog_ref, e2_ref, *rest, reverse):
    if reverse:
        yf_ref, z_ref, dexp_ref, nw_ref, o_ref, st_ref, y_ref = rest
    else:
        o_ref, st_ref = rest
        y_ref = o_ref
    q = SSD_CHUNK
    off = SSD_HEADS if reverse else 0

    @pl.when(pl.program_id(1) == 0)
    def _():
        st_ref[...] = jnp.zeros(st_ref.shape, F32)

    x_raw = dt_ref[...] + bias_ref[...]
    dt = jnp.maximum(x_raw, 0.0) + jnp.log1p(jnp.exp(-jnp.abs(x_raw)))
    da = dt * (-jnp.exp(alog_ref[...]))
    ii = lax.broadcasted_iota(jnp.int32, (q, q), 0)
    jj = lax.broadcasted_iota(jnp.int32, (q, q), 1)
    tri = (jj >= ii) if reverse else (jj <= ii)
    cs = jnp.dot(tri.astype(F32), da, precision=lax.Precision.HIGHEST, preferred_element_type=F32)
    total = cs[0:1, :] if reverse else cs[q - 1:q, :]
    cs2 = cs * LOG2E
    row_t = (cs2 - jnp.log2(dt)).T
    w_end = dt * jnp.exp(total - cs)
    chunk_decay = jnp.broadcast_to(jnp.exp(total), (2 * SUBLANE, LANE))
    spread = _dot(_split_hi_lo(jnp.concatenate([w_end, chunk_decay], axis=0)), e2_ref[...])
    w_exp = spread[0:q, :]
    decay_exp = spread[q:q + 1, :]

    lane = lax.broadcasted_iota(jnp.int32, (q, LANE), 1)
    for g in range(SSD_GROUPS):
        b_g = xbc_ref[:, SSD_INNER + g * SSD_STATE:SSD_INNER + (g + 1) * SSD_STATE]
        c_g = xbc_ref[:, SSD_INNER + SSD_BC + g * SSD_STATE:SSD_INNER + SSD_BC + (g + 1) * SSD_STATE]
        cb = _dot_nt(c_g.astype(BF16), b_g.astype(BF16))
        gs = slice(g * SSD_GROUP_W, (g + 1) * SSD_GROUP_W)
        st_old = st_ref[:, gs]
        st_bf = st_old.astype(BF16)
        x_g = xbc_ref[:, gs]
        for k in range(SSD_GROUP_W // LANE):
            ls = slice(k * LANE, (k + 1) * LANE)
            rhs = jnp.concatenate([x_g[:, ls].astype(BF16), st_bf[:, ls]], axis=0)
            lhs = []
            for h in (g * SSD_HPG + 2 * k, g * SSD_HPG + 2 * k + 1):
                col = off + h
                cs_i = jnp.broadcast_to(cs2[:, col:col + 1], (q, q))
                m = jnp.exp2(jnp.where(tri, cs_i - row_t[col:col + 1, :], -jnp.inf)) * cb
                c_scaled = c_g * jnp.exp2(cs_i)
                lhs.append(jnp.concatenate([m.astype(BF16), c_scaled.astype(BF16)], axis=1))
            out = _dot(jnp.concatenate(lhs, axis=0), rhs)
            y_pair = jnp.where(lane < SSD_HEADDIM, out[0:q, :], out[q:2 * q, :])
            cols = slice(g * SSD_GROUP_W + k * LANE, g * SSD_GROUP_W + (k + 1) * LANE)
            if reverse:
                y_pair = y_pair + yf_ref[:, cols] + dexp_ref[:, cols] * x_g[:, ls]
                zz = z_ref[:, cols]
                y_pair = y_pair * _silu(zz)
            y_ref[:, cols] = y_pair
        xw = (x_g * w_exp[:, gs]).astype(BF16)
        st_ref[:, gs] = st_old * decay_exp[:, gs] + _dot(b_g.T.astype(BF16), xw)

    if reverse:
        for g in range(SSD_GROUPS):
            gs = slice(g * SSD_GROUP_W, (g + 1) * SSD_GROUP_W)
            y = y_ref[:, gs]
            y = y * lax.rsqrt(jnp.mean(y * y, axis=-1, keepdims=True) + EPS) * nw_ref[:, gs]
            o_ref[:, gs] = y.astype(o_ref.dtype)


def _ssd_scan(xbc_act, dt_raw, bias_pad, alog_pad, e2, reverse, extra=()):
    ctx_chunks = CTX_LEN // SSD_CHUNK
    lat_chunks = SEQ // SSD_CHUNK
    ctx_base = N_LAT // SSD_CHUNK
    steps = ctx_chunks + lat_chunks

    def chunk(b, s):
        if reverse:
            c_ctx, c_lat = ctx_chunks - 1 - s, steps - 1 - s
        else:
            c_ctx, c_lat = s, s - ctx_chunks
        return jnp.where(s < ctx_chunks, ctx_base + b * ctx_chunks + c_ctx, b * lat_chunks + c_lat)

    rows = lambda w: pl.BlockSpec((SSD_CHUNK, w), lambda b, s: (chunk(b, s), 0))
    const = lambda shape: pl.BlockSpec(shape, lambda b, s: (0, 0))
    in_specs = [rows(SSD_CONV_DIM), rows(LANE), const((1, LANE)), const((1, LANE)), const(e2.shape)]
    args = [xbc_act, dt_raw, bias_pad, alog_pad, e2]
    if reverse:
        y_f, z, d_exp, norm_w = extra
        in_specs += [rows(SSD_INNER), rows(SSD_INNER), const((1, SSD_INNER)), const((1, SSD_INNER))]
        args += [y_f, z, d_exp, norm_w]
    scratch = [pltpu.VMEM((SSD_STATE, SSD_INNER), F32)]
    if reverse:
        scratch.append(pltpu.VMEM((SSD_CHUNK, SSD_INNER), F32))
    return pl.pallas_call(
        functools.partial(_ssd_scan_kernel, reverse=reverse),
        out_shape=jax.ShapeDtypeStruct((N_TOK, SSD_INNER), BF16 if reverse else F32),
        grid=(BATCH, steps),
        in_specs=in_specs,
        out_specs=rows(SSD_INNER),
        scratch_shapes=scratch,
        compiler_params=_params(("parallel", "arbitrary")),
        name="ssd_scan_bwd" if reverse else "ssd_scan_fwd",
    )(*args)


def _head_spread_matrix(off):
    e = np.zeros((LANE, SSD_INNER), np.float32)
    for h in range(SSD_HEADS):
        e[off + h, h * SSD_HEADDIM:(h + 1) * SSD_HEADDIM] = 1.0
    return jnp.asarray(np.concatenate([e, e], axis=0), BF16)


def _ssd_mixer(h, mods_l, g, w_in, conv_w, conv_b, dt_bias, a_log, d_skip, norm_w, w_out):
    wz = w_in[:, :SSD_INNER].astype(BF16)
    wxbc = w_in[:, SSD_INNER:SSD_INNER + SSD_CONV_DIM].astype(BF16)
    n_dt = 2 * SSD_HEADS
    wdt = jnp.pad(w_in[:, SSD_INNER + SSD_CONV_DIM:], ((0, 0), (0, LANE - n_dt))).astype(BF16)
    z, xbc_act, dt_raw = _ssd_in(h, mods_l, g, wz, wxbc, wdt, conv_w, conv_b)
    bias_pad = jnp.pad(dt_bias.reshape(1, n_dt), ((0, 0), (0, LANE - n_dt)))
    alog_pad = jnp.pad(a_log.reshape(1, n_dt), ((0, 0), (0, LANE - n_dt)))
    y_f = _ssd_scan(xbc_act, dt_raw, bias_pad, alog_pad, _head_spread_matrix(0), False)
    d_exp = jnp.repeat(d_skip, SSD_HEADDIM).reshape(1, SSD_INNER)
    y = _ssd_scan(xbc_act, dt_raw, bias_pad, alog_pad, _head_spread_matrix(SSD_HEADS), True,
                  extra=(y_f, z, d_exp, norm_w.reshape(1, SSD_INNER)))
    return _out_proj(h, mods_l, y, w_out.astype(BF16), "ssd_out_proj")


def _shortconv_kernel(h_ref, mod_ref, g_ref, wb_ref, wc_ref, wh_ref, cw_ref, wo_ref, o_ref, v_ref):
    h = h_ref[...]
    u = _norm_mod(h, g_ref[...], mod_ref[3:4, :], mod_ref[4:5, :]).astype(BF16)
    v = _dot(u, wc_ref[...]) * _dot(u, wh_ref[...])
    v_ref[0:HALO, :] = jnp.zeros((HALO, D_MODEL), F32)
    v_ref[HALO:HALO + ROW_TILE, :] = v
    v_ref[HALO + ROW_TILE:, :] = jnp.zeros((HALO, D_MODEL), F32)
    is_ctx = pl.program_id(0) >= N_LAT // ROW_TILE
    row_len = jnp.where(is_ctx, CTX_LEN, GRID_W)
    pos = lax.broadcasted_iota(jnp.int32, (ROW_TILE, 1), 0) & (row_len - 1)
    v_prev = jnp.where(pos == 0, 0.0, v_ref[HALO - 1:HALO - 1 + ROW_TILE, :])
    v_next = jnp.where(pos == row_len - 1, 0.0, v_ref[HALO + 1:HALO + 1 + ROW_TILE, :])
    conv = cw_ref[0:1, :] * v_prev + cw_ref[1:2, :] * v + cw_ref[2:3, :] * v_next
    y = _dot((_dot(u, wb_ref[...]) * conv).astype(BF16), wo_ref[...])
    o_ref[...] = h + mod_ref[5:6, :] * y


def _shortconv_mixer(h, mods_l, g, w_in, conv_w, w_out):
    wb, wc, wh = (w_in[:, k * D_MODEL:(k + 1) * D_MODEL].astype(BF16) for k in range(3))
    row = lambda i: (i, 0)
    const = lambda i: (0, 0)
    wspec = pl.BlockSpec((D_MODEL, D_MODEL), const)
    return pl.pallas_call(
        _shortconv_kernel,
        out_shape=jax.ShapeDtypeStruct((N_TOK, D_MODEL), F32),
        grid=(N_TOK // ROW_TILE,),
        in_specs=[
            pl.BlockSpec((ROW_TILE, D_MODEL), row),
            pl.BlockSpec((None, N_MOD, D_MODEL), lambda i: (_mod_row(i, ROW_TILE), 0, 0)),
            pl.BlockSpec((1, D_MODEL), const),
            wspec, wspec, wspec,
            pl.BlockSpec(conv_w.shape, const),
            wspec,
        ],
        out_specs=pl.BlockSpec((ROW_TILE, D_MODEL), row),
        scratch_shapes=[pltpu.VMEM((ROW_TILE + 2 * HALO, D_MODEL), F32)],
        compiler_params=_params(("parallel",)),
        name="shortconv_mixer",
    )(h, mods_l, g.reshape(1, D_MODEL), wb, wc, wh, conv_w, w_out.astype(BF16))


def _gmlp_kernel(h_ref, mod_ref, g_ref, wu_ref, wv_ref, vn_ref, ws_ref, bs_ref, wo_ref, o_ref, a_ref):
    h = h_ref[...]
    u = _norm_mod(h, g_ref[...], mod_ref[3:4, :], mod_ref[4:5, :]).astype(BF16)
    zu = jax.nn.gelu(_dot(u, wu_ref[...]), approximate=True)
    zv = jax.nn.gelu(_dot(u, wv_ref[...]), approximate=True)
    zc = zv - jnp.mean(zv, axis=-1, keepdims=True)
    zv = zc * lax.rsqrt(jnp.mean(zc * zc, axis=-1, keepdims=True) + EPS) * vn_ref[...]
    zv = zv.astype(BF16)
    for c in range(ROW_TILE // GM_CHUNK):
        rs = slice(c * GM_CHUNK, (c + 1) * GM_CHUNK)
        for g in range(GM_GROUPS):
            gs = slice(g * GM_GROUP_DIM, (g + 1) * GM_GROUP_DIM)
            s = _dot(ws_ref[g], zv[rs, gs]) + bs_ref[:, g:g + 1]
            a_ref[rs, gs] = (zu[rs, gs] * s).astype(BF16)
    o_ref[...] = h + mod_ref[5:6, :] * _dot(a_ref[...], wo_ref[...])


def _gmlp_mixer(h, mods_l, g, w_in, v_norm, w_s, b_s, w_out):
    wu = w_in[:, :GM_DIM].astype(BF16)
    wv = w_in[:, GM_DIM:].astype(BF16)
    row = lambda i: (i, 0)
    const = lambda i: (0, 0)
    return pl.pallas_call(
        _gmlp_kernel,
        out_shape=jax.ShapeDtypeStruct((N_TOK, D_MODEL), F32),
        grid=(N_TOK // ROW_TILE,),
        in_specs=[
            pl.BlockSpec((ROW_TILE, D_MODEL), row),
            pl.BlockSpec((None, N_MOD, D_MODEL), lambda i: (_mod_row(i, ROW_TILE), 0, 0)),
            pl.BlockSpec((1, D_MODEL), const),
            pl.BlockSpec((D_MODEL, GM_DIM), const),
            pl.BlockSpec((D_MODEL, GM_DIM), const),
            pl.BlockSpec((1, GM_DIM), const),
            pl.BlockSpec(w_s.shape, lambda i: (0, 0, 0)),
            pl.BlockSpec((GM_CHUNK, GM_GROUPS), const),
            pl.BlockSpec((GM_DIM, D_MODEL), const),
        ],
        out_specs=pl.BlockSpec((ROW_TILE, D_MODEL), row),
        scratch_shapes=[pltpu.VMEM((ROW_TILE, GM_DIM), BF16)],
        compiler_params=_params(("parallel",)),
        name="gmlp_mixer",
    )(h, mods_l, g.reshape(1, D_MODEL), wu, wv, v_norm.reshape(1, GM_DIM), w_s.astype(BF16),
      b_s.T, w_out.astype(BF16))


def kernel(x, c, ctx, c_ctx, ada_w, ada_b, norm_g, ffn_wg, ffn_wu, ffn_wd, ssd_in, ssd_conv_w, ssd_conv_b, ssd_dt_bias, ssd_a_log, ssd_d, ssd_norm, ssd_out, sc_in, sc_conv, sc_out, gm_in, gm_vnorm, gm_ws, gm_bs, gm_out, final_norm):
    h = jnp.concatenate([x.reshape(N_LAT, D_MODEL), ctx.reshape(N_CTX, D_MODEL)], axis=0)
    mods = _modulations(c, c_ctx, ada_w, ada_b)
    for i in range(DEPTH):
        last = i == DEPTH - 1
        m = mods[i]
        h = _half_ffn(h, m, norm_g[i, 0], ffn_wg[i, 0].astype(BF16), ffn_wu[i, 0].astype(BF16),
                      ffn_wd[i, 0].astype(BF16), 0, N_TOK)
        kind, j = i % 3, i // 3
        if kind == 0:
            h = _ssd_mixer(h, m, norm_g[i, 1], ssd_in[j], ssd_conv_w[j], ssd_conv_b[j], ssd_dt_bias[j],
                           ssd_a_log[j], ssd_d[j], ssd_norm[j], ssd_out[j])
        elif kind == 1:
            h = _shortconv_mixer(h, m, norm_g[i, 1], sc_in[j], sc_conv[j], sc_out[j])
        else:
            h = _gmlp_mixer(h, m, norm_g[i, 1], gm_in[j], gm_vnorm[j], gm_ws[j], gm_bs[j], gm_out[j])
        h = _half_ffn(h, m, norm_g[i, 2], ffn_wg[i, 1].astype(BF16), ffn_wu[i, 1].astype(BF16),
                      ffn_wd[i, 1].astype(BF16), 6, N_LAT if last else N_TOK,
                      final_g=final_norm if last else None)
    return h.reshape(BATCH, SEQ, D_MODEL)
```

```python
import functools

import jax
import jax.numpy as jnp
import numpy as np
from jax import lax
from jax.experimental import pallas as pl
from jax.experimental.pallas import tpu as pltpu

F32 = jnp.float32
BF16 = jnp.bfloat16

D_MODEL = 1024
BATCH = 2
SEQ = 8192
DEPTH = 4
GRID_W = 64
CTX_LEN = 256
N_MOD = 9
EPS = 1e-6
LOG2E = 1.4426950408889634
FFN_DIM = 2816

SSD_INNER = 2048
SSD_HEADDIM = 64
SSD_HEADS = 32
SSD_STATE = 128
SSD_GROUPS = 4
SSD_HPG = 8
SSD_CONV = 5
SSD_CHUNK = 128
SSD_BC = SSD_GROUPS * SSD_STATE
SSD_CONV_DIM = SSD_INNER + 2 * SSD_BC
SSD_GROUP_W = SSD_HPG * SSD_HEADDIM

GM_CHUNK = 128
GM_DIM = 2048
GM_GROUPS = 8
GM_GROUP_DIM = 256

N_LAT = BATCH * SEQ
N_CTX = BATCH * CTX_LEN
N_TOK = N_LAT + N_CTX
CTX_MOD_ROW = BATCH

LANE = 128
SUBLANE = 8
VMEM_LIMIT = 56 * 1024 * 1024

FFN_TM = 512
FFN_FC = 256
ROW_TILE = 256
HALO = SUBLANE
CONV_COLS = 256
SCAN_CHUNKS = 2
CONV_GROUPS = 8


def _dot(a, b):
    return jnp.dot(a, b, preferred_element_type=F32)


def _dot_nt(a, b):
    return lax.dot_general(a, b, (((1,), (1,)), ((), ())), preferred_element_type=F32)


def _mxu_weight(w):
    tiles = -(-w.shape[1] // LANE)
    pad = (tiles + (tiles + 1) % 2) * LANE - w.shape[1]
    return jnp.pad(w.astype(BF16), ((0, 0), (0, pad)))


def _silu(x):
    hx = 0.5 * x
    return hx + hx * jnp.tanh(hx)


def _mod_row(tile, tile_rows):
    return jnp.minimum((tile * tile_rows) // SEQ, CTX_MOD_ROW)


def _norm_mod(h, g, shift, scale):
    y = h * lax.rsqrt(jnp.mean(h * h, axis=-1, keepdims=True) + EPS) * g
    return y * (1.0 + scale) + shift


def _params(sem):
    return pltpu.CompilerParams(dimension_semantics=sem, vmem_limit_bytes=VMEM_LIMIT)


def _mod_kernel(ct_ref, w_ref, b_ref, o_ref, s_ref):
    n_vec = BATCH + 1
    tn = w_ref.shape[-1]

    @pl.when(jnp.logical_and(pl.program_id(0) == 0, pl.program_id(1) == 0))
    def _():
        s = _silu(ct_ref[...])
        for r in range(n_vec):
            s_ref[r] = jnp.broadcast_to(s[:, r:r + 1], (D_MODEL, LANE))

    def body(k, accs):
        r0 = pl.multiple_of(k * SUBLANE, SUBLANE)
        w8 = w_ref[pl.ds(r0, SUBLANE), :]
        out = []
        for r, a in enumerate(accs):
            s8 = s_ref[r, pl.ds(r0, SUBLANE), :]
            out.append(a + w8 * jnp.concatenate([s8] * (tn // LANE), axis=1))
        return tuple(out)

    zero = jnp.zeros((SUBLANE, tn), F32)
    accs = lax.fori_loop(0, D_MODEL // SUBLANE, body, (zero,) * n_vec, unroll=4)
    o_ref[...] = jnp.zeros(o_ref.shape, F32)
    for r, a in enumerate(accs):
        o_ref[r:r + 1, :] = jnp.sum(a, axis=0, keepdims=True) + b_ref[...]


def _modulations(c, c_ctx, ada_w, ada_b):
    ct = jnp.concatenate([c, c_ctx[None, :], jnp.zeros((SUBLANE - BATCH - 1, D_MODEL), F32)], axis=0).T
    tn = 1024
    n_out = N_MOD * D_MODEL
    out = pl.pallas_call(
        _mod_kernel,
        out_shape=jax.ShapeDtypeStruct((DEPTH, SUBLANE, n_out), F32),
        grid=(DEPTH, n_out // tn),
        in_specs=[
            pl.BlockSpec((D_MODEL, SUBLANE), lambda l, j: (0, 0)),
            pl.BlockSpec((None, D_MODEL, tn), lambda l, j: (l, 0, j)),
            pl.BlockSpec((None, 1, tn), lambda l, j: (l, 0, j)),
        ],
        out_specs=pl.BlockSpec((None, SUBLANE, tn), lambda l, j: (l, 0, j)),
        scratch_shapes=[pltpu.VMEM((BATCH + 1, D_MODEL, LANE), F32)],
        compiler_params=_params(("arbitrary", "arbitrary")),
        name="adaln_mod",
    )(ct, ada_w, ada_b.reshape(DEPTH, 1, n_out))
    return out.reshape(DEPTH, SUBLANE, N_MOD, D_MODEL)


def _ffn_kernel(h_ref, mod_ref, g_ref, wg_ref, wu_ref, wd_ref, *rest, mod_base, final):
    if final:
        fg_ref, o_ref, a_ref = rest
    else:
        o_ref, a_ref = rest
    h = h_ref[...]
    u = _norm_mod(h, g_ref[...], mod_ref[mod_base:mod_base + 1, :],
                  mod_ref[mod_base + 1:mod_base + 2, :]).astype(BF16)
    for c0 in range(0, FFN_DIM, FFN_FC):
        cs = slice(c0, c0 + FFN_FC)
        a_ref[:, cs] = (_silu(_dot(u, wg_ref[:, cs])) * _dot(u, wu_ref[:, cs])).astype(BF16)
    out = h + 0.5 * mod_ref[mod_base + 2:mod_base + 3, :] * _dot(a_ref[...], wd_ref[:, 0:D_MODEL])
    if final:
        out = out * lax.rsqrt(jnp.mean(out * out, axis=-1, keepdims=True) + EPS) * fg_ref[...]
    o_ref[...] = out


def _resident(shape):
    return pl.BlockSpec(shape, lambda *_: (0,) * len(shape), pipeline_mode=pl.Buffered(1))


def _half_ffn(h, mods_l, g, wg, wu, wd, mod_base, n_rows, final_g=None):
    final = final_g is not None
    row = lambda i: (i, 0)
    in_specs = [
        pl.BlockSpec((FFN_TM, D_MODEL), row),
        pl.BlockSpec((None, N_MOD, D_MODEL), lambda i: (_mod_row(i, FFN_TM), 0, 0)),
        _resident((1, D_MODEL)),
        _resident(wg.shape), _resident(wu.shape), _resident(wd.shape),
    ]
    args = [h, mods_l, g.reshape(1, D_MODEL), wg, wu, wd]
    if final:
        in_specs.append(_resident((1, D_MODEL)))
        args.append(final_g.reshape(1, D_MODEL))
    return pl.pallas_call(
        functools.partial(_ffn_kernel, mod_base=mod_base, final=final),
        out_shape=jax.ShapeDtypeStruct((n_rows, D_MODEL), F32),
        grid=(n_rows // FFN_TM,),
        in_specs=in_specs,
        out_specs=pl.BlockSpec((FFN_TM, D_MODEL), row),
        scratch_shapes=[pltpu.VMEM((FFN_TM, FFN_DIM), BF16)],
        compiler_params=_params(("parallel",)),
        name="half_ffn_final" if final else "half_ffn",
    )(*args)


def _out_proj_kernel(h_ref, mod_ref, a_ref, w_ref, o_ref):
    o_ref[...] = h_ref[...] + mod_ref[5:6, :] * _dot(a_ref[...], w_ref[:, 0:D_MODEL])


def _out_proj(h, mods_l, a, w, name):
    tm = 512
    row = lambda i: (i, 0)
    return pl.pallas_call(
        _out_proj_kernel,
        out_shape=jax.ShapeDtypeStruct((N_TOK, D_MODEL), F32),
        grid=(N_TOK // tm,),
        in_specs=[
            pl.BlockSpec((tm, D_MODEL), row),
            pl.BlockSpec((None, N_MOD, D_MODEL), lambda i: (_mod_row(i, tm), 0, 0)),
            pl.BlockSpec((tm, a.shape[1]), row),
            _resident(w.shape),
        ],
        out_specs=pl.BlockSpec((tm, D_MODEL), row),
        compiler_params=_params(("parallel",)),
        name=name,
    )(h, mods_l, a, w)


def _seq_edges(tile):
    lat_tiles = SEQ // ROW_TILE
    is_ctx = tile >= BATCH * lat_tiles
    pos = tile % lat_tiles
    return jnp.logical_or(is_ctx, pos == 0), jnp.logical_or(is_ctx, pos == lat_tiles - 1)


def _ssd_in_kernel(hp_ref, h_ref, hn_ref, mod_ref, g_ref, wz_ref, wx_ref, wdt_ref, cw_ref, cb_ref,
                   z_ref, x_ref, dt_ref):
    first, last = _seq_edges(pl.program_id(0))
    g, shift, scale = g_ref[...], mod_ref[3:4, :], mod_ref[4:5, :]
    u = _norm_mod(h_ref[...], g, shift, scale)
    u_ext = jnp.concatenate([_norm_mod(hp_ref[...], g, shift, scale), u,
                             _norm_mod(hn_ref[...], g, shift, scale)], axis=0).astype(BF16)
    ub = u.astype(BF16)
    n_grp = (ROW_TILE + 2 * HALO) // SUBLANE
    half = SSD_CONV // 2
    sub = lax.broadcasted_iota(jnp.int32, (1, SUBLANE, 1), 1)

    def project(blk):
        cols = slice(blk * CONV_COLS, (blk + 1) * CONV_COLS)
        e = _dot(u_ext, wx_ref[:, cols])
        e = jnp.concatenate([jnp.where(first, 0.0, e[0:HALO, :]), e[HALO:HALO + ROW_TILE, :],
                             jnp.where(last, 0.0, e[HALO + ROW_TILE:, :])], axis=0)
        return e.reshape(n_grp, SUBLANE, CONV_COLS)

    def conv(blk, e3):
        cols = slice(blk * CONV_COLS, (blk + 1) * CONV_COLS)
        for g0 in range(0, ROW_TILE // SUBLANE, CONV_GROUPS):
            ein = e3[g0:g0 + CONV_GROUPS + 2]
            acc = cb_ref[:, cols] + cw_ref[half:half + 1, cols] * ein[1:CONV_GROUPS + 1]
            for k in range(SSD_CONV):
                s = k - half
                if s == 0:
                    continue
                r3 = pltpu.roll(ein, (-s) % SUBLANE, axis=1)
                if s > 0:
                    shifted = jnp.where(sub < SUBLANE - s, r3[1:CONV_GROUPS + 1], r3[2:CONV_GROUPS + 2])
                else:
                    shifted = jnp.where(sub >= -s, r3[1:CONV_GROUPS + 1], r3[0:CONV_GROUPS])
                acc = acc + cw_ref[k:k + 1, cols] * shifted
            rows = slice(g0 * SUBLANE, (g0 + CONV_GROUPS) * SUBLANE)
            x_ref[rows, cols] = _silu(acc).reshape(CONV_GROUPS * SUBLANE, CONV_COLS)

    n_x = SSD_CONV_DIM // CONV_COLS
    n_z = SSD_INNER // CONV_COLS
    e3 = project(0)
    for blk in range(n_x):
        e3_next = project(blk + 1) if blk + 1 < n_x else None
        if blk < n_z:
            cols = slice(blk * CONV_COLS, (blk + 1) * CONV_COLS)
            z_ref[:, cols] = _dot(ub, wz_ref[:, cols])
        elif blk == n_z:
            dt_ref[...] = _dot(ub, wdt_ref[...])
        conv(blk, e3)
        e3 = e3_next


def _ssd_in(h, mods_l, g, wz, wx, wdt, conv_w, conv_b):
    per = ROW_TILE // HALO
    n_halo = N_TOK // HALO
    row = lambda i: (i, 0)
    widths = (SSD_INNER, SSD_CONV_DIM, LANE)
    return pl.pallas_call(
        _ssd_in_kernel,
        out_shape=[jax.ShapeDtypeStruct((N_TOK, n), F32) for n in widths],
        grid=(N_TOK // ROW_TILE,),
        in_specs=[
            pl.BlockSpec((HALO, D_MODEL), lambda i: (jnp.maximum(i * per - 1, 0), 0)),
            pl.BlockSpec((ROW_TILE, D_MODEL), row),
            pl.BlockSpec((HALO, D_MODEL), lambda i: (jnp.minimum((i + 1) * per, n_halo - 1), 0)),
            pl.BlockSpec((None, N_MOD, D_MODEL), lambda i: (_mod_row(i, ROW_TILE), 0, 0)),
            _resident((1, D_MODEL)),
            _resident(wz.shape), _resident(wx.shape), _resident(wdt.shape),
            _resident(conv_w.shape), _resident((1, SSD_CONV_DIM)),
        ],
        out_specs=[pl.BlockSpec((ROW_TILE, n), row) for n in widths],
        compiler_params=_params(("parallel",)),
        name="ssd_in_proj",
    )(h, h, h, mods_l, g.reshape(1, D_MODEL), wz, wx, wdt, conv_w, conv_b.reshape(1, SSD_CONV_DIM))


def _split_hi_lo(x):
    hi = x.astype(BF16)
    lo = (x - hi.astype(F32)).astype(BF16)
    return jnp.concatenate([hi, lo], axis=1)


def _ssd_scan_kernel(xbc_ref, dt_ref, bias_ref, alog_ref, e2_ref, *rest, reverse):
    if reverse:
        yf_ref, z_ref, dexp_ref, nw_ref, o_ref, st_ref, y_ref = rest
    else:
        o_ref, st_ref = rest
        y_ref = o_ref
    q = SSD_CHUNK
    off = SSD_HEADS if reverse else 0

    @pl.when(pl.program_id(1) == 0)
    def _():
        st_ref[...] = jnp.zeros(st_ref.shape, F32)

    ii = lax.broadcasted_iota(jnp.int32, (q, q), 0)
    jj = lax.broadcasted_iota(jnp.int32, (q, q), 1)
    tri = (jj >= ii) if reverse else (jj <= ii)
    lane = lax.broadcasted_iota(jnp.int32, (q, LANE), 1)
    neg_a = -jnp.exp(alog_ref[...])

    def chunk(r0):
        rows = slice(r0, r0 + q)
        x_raw = dt_ref[rows, :] + bias_ref[...]
        dt = jnp.maximum(x_raw, 0.0) + jnp.log1p(jnp.exp(-jnp.abs(x_raw)))
        da = dt * neg_a
        cs = jnp.dot(tri.astype(F32), da, precision=lax.Precision.HIGHEST, preferred_element_type=F32)
        total = cs[0:1, :] if reverse else cs[q - 1:q, :]
        cs2 = cs * LOG2E
        row_t = (cs2 - jnp.log2(dt)).T
        w_end = dt * jnp.exp(total - cs)
        chunk_decay = jnp.broadcast_to(jnp.exp(total), (2 * SUBLANE, LANE))
        spread = _dot(_split_hi_lo(jnp.concatenate([w_end, chunk_decay], axis=0)), e2_ref[...])
        w_exp = spread[0:q, :]
        decay_exp = spread[q:q + 1, :]

        for g in range(SSD_GROUPS):
            b_g = xbc_ref[rows, SSD_INNER + g * SSD_STATE:SSD_INNER + (g + 1) * SSD_STATE]
            c_lo = SSD_INNER + SSD_BC + g * SSD_STATE
            c_g = xbc_ref[rows, c_lo:c_lo + SSD_STATE]
            cb = _dot_nt(c_g.astype(BF16), b_g.astype(BF16))
            gs = slice(g * SSD_GROUP_W, (g + 1) * SSD_GROUP_W)
            st_old = st_ref[:, gs]
            st_bf = st_old.astype(BF16)
            x_g = xbc_ref[rows, gs]
            for k in range(SSD_GROUP_W // LANE):
                ls = slice(k * LANE, (k + 1) * LANE)
                rhs = jnp.concatenate([x_g[:, ls].astype(BF16), st_bf[:, ls]], axis=0)
                lhs = []
                for h in (g * SSD_HPG + 2 * k, g * SSD_HPG + 2 * k + 1):
                    col = off + h
                    cs_i = jnp.broadcast_to(cs2[:, col:col + 1], (q, q))
                    m = jnp.exp2(jnp.where(tri, cs_i - row_t[col:col + 1, :], -jnp.inf)) * cb
                    c_scaled = c_g * jnp.exp2(cs_i)
                    lhs.append(jnp.concatenate([m.astype(BF16), c_scaled.astype(BF16)], axis=1))
                out = _dot(jnp.concatenate(lhs, axis=0), rhs)
                y_pair = jnp.where(lane < SSD_HEADDIM, out[0:q, :], out[q:2 * q, :])
                cols = slice(g * SSD_GROUP_W + k * LANE, g * SSD_GROUP_W + (k + 1) * LANE)
                if reverse:
                    y_pair = y_pair + yf_ref[rows, cols] + dexp_ref[:, cols] * x_g[:, ls]
                    y_pair = y_pair * _silu(z_ref[rows, cols])
                y_ref[rows, cols] = y_pair
            xw = (x_g * w_exp[:, gs]).astype(BF16)
            st_ref[:, gs] = st_old * decay_exp[:, gs] + _dot(b_g.T.astype(BF16), xw)

        if reverse:
            for g in range(SSD_GROUPS):
                gs = slice(g * SSD_GROUP_W, (g + 1) * SSD_GROUP_W)
                y = y_ref[rows, gs]
                y = y * lax.rsqrt(jnp.mean(y * y, axis=-1, keepdims=True) + EPS) * nw_ref[:, gs]
                o_ref[rows, gs] = y.astype(o_ref.dtype)

    order = range(SCAN_CHUNKS - 1, -1, -1) if reverse else range(SCAN_CHUNKS)
    for c in order:
        chunk(c * q)


def _ssd_scan(xbc_act, dt_raw, bias_pad, alog_pad, e2, reverse, extra=()):
    block = SCAN_CHUNKS * SSD_CHUNK
    lat_steps = SEQ // block
    ctx_base = N_LAT // block
    steps = 1 + lat_steps

    def blk(b, s):
        lat = (lat_steps - s) if reverse else (s - 1)
        return jnp.where(s == 0, ctx_base + b, b * lat_steps + lat)

    rows = lambda w: pl.BlockSpec((block, w), lambda b, s: (blk(b, s), 0))
    const = lambda shape: pl.BlockSpec(shape, lambda b, s: (0, 0))
    in_specs = [rows(SSD_CONV_DIM), rows(LANE), const((1, LANE)), const((1, LANE)), const(e2.shape)]
    args = [xbc_act, dt_raw, bias_pad, alog_pad, e2]
    if reverse:
        y_f, z, d_exp, norm_w = extra
        in_specs += [rows(SSD_INNER), rows(SSD_INNER), const((1, SSD_INNER)), const((1, SSD_INNER))]
        args += [y_f, z, d_exp, norm_w]
    scratch = [pltpu.VMEM((SSD_STATE, SSD_INNER), F32)]
    if reverse:
        scratch.append(pltpu.VMEM((block, SSD_INNER), F32))
    return pl.pallas_call(
        functools.partial(_ssd_scan_kernel, reverse=reverse),
        out_shape=jax.ShapeDtypeStruct((N_TOK, SSD_INNER), BF16 if reverse else F32),
        grid=(BATCH, steps),
        in_specs=in_specs,
        out_specs=rows(SSD_INNER),
        scratch_shapes=scratch,
        compiler_params=_params(("parallel", "arbitrary")),
        name="ssd_scan_bwd" if reverse else "ssd_scan_fwd",
    )(*args)


def _head_spread_matrix(off):
    e = np.zeros((LANE, SSD_INNER), np.float32)
    for h in range(SSD_HEADS):
        e[off + h, h * SSD_HEADDIM:(h + 1) * SSD_HEADDIM] = 1.0
    return jnp.asarray(np.concatenate([e, e], axis=0), BF16)


def _ssd_mixer(h, mods_l, g, w_in, conv_w, conv_b, dt_bias, a_log, d_skip, norm_w, w_out):
    wz = _mxu_weight(w_in[:, :SSD_INNER])
    wxbc = _mxu_weight(w_in[:, SSD_INNER:SSD_INNER + SSD_CONV_DIM])
    n_dt = 2 * SSD_HEADS
    wdt = _mxu_weight(w_in[:, SSD_INNER + SSD_CONV_DIM:])
    z, xbc_act, dt_raw = _ssd_in(h, mods_l, g, wz, wxbc, wdt, conv_w, conv_b)
    bias_pad = jnp.pad(dt_bias.reshape(1, n_dt), ((0, 0), (0, LANE - n_dt)))
    alog_pad = jnp.pad(a_log.reshape(1, n_dt), ((0, 0), (0, LANE - n_dt)))
    y_f = _ssd_scan(xbc_act, dt_raw, bias_pad, alog_pad, _head_spread_matrix(0), False)
    d_exp = jnp.repeat(d_skip, SSD_HEADDIM).reshape(1, SSD_INNER)
    y = _ssd_scan(xbc_act, dt_raw, bias_pad, alog_pad, _head_spread_matrix(SSD_HEADS), True,
                  extra=(y_f, z, d_exp, norm_w.reshape(1, SSD_INNER)))
    return _out_proj(h, mods_l, y, _mxu_weight(w_out), "ssd_out_proj")


def _shortconv_kernel(h_ref, mod_ref, g_ref, wb_ref, wc_ref, wh_ref, cw_ref, wo_ref, o_ref, v_ref):
    h = h_ref[...]
    u = _norm_mod(h, g_ref[...], mod_ref[3:4, :], mod_ref[4:5, :]).astype(BF16)
    v = _dot(u, wc_ref[:, 0:D_MODEL]) * _dot(u, wh_ref[:, 0:D_MODEL])
    v_ref[0:HALO, :] = jnp.zeros((HALO, D_MODEL), F32)
    v_ref[HALO:HALO + ROW_TILE, :] = v
    v_ref[HALO + ROW_TILE:, :] = jnp.zeros((HALO, D_MODEL), F32)
    is_ctx = pl.program_id(0) >= N_LAT // ROW_TILE
    row_len = jnp.where(is_ctx, CTX_LEN, GRID_W)
    pos = lax.broadcasted_iota(jnp.int32, (ROW_TILE, 1), 0) & (row_len - 1)
    v_prev = jnp.where(pos == 0, 0.0, v_ref[HALO - 1:HALO - 1 + ROW_TILE, :])
    v_next = jnp.where(pos == row_len - 1, 0.0, v_ref[HALO + 1:HALO + 1 + ROW_TILE, :])
    conv = cw_ref[0:1, :] * v_prev + cw_ref[1:2, :] * v + cw_ref[2:3, :] * v_next
    y = _dot((_dot(u, wb_ref[:, 0:D_MODEL]) * conv).astype(BF16), wo_ref[:, 0:D_MODEL])
    o_ref[...] = h + mod_ref[5:6, :] * y


def _shortconv_mixer(h, mods_l, g, w_in, conv_w, w_out):
    wb, wc, wh = (_mxu_weight(w_in[:, k * D_MODEL:(k + 1) * D_MODEL]) for k in range(3))
    row = lambda i: (i, 0)
    const = lambda i: (0, 0)
    wspec = _resident(wb.shape)
    return pl.pallas_call(
        _shortconv_kernel,
        out_shape=jax.ShapeDtypeStruct((N_TOK, D_MODEL), F32),
        grid=(N_TOK // ROW_TILE,),
        in_specs=[
            pl.BlockSpec((ROW_TILE, D_MODEL), row),
            pl.BlockSpec((None, N_MOD, D_MODEL), lambda i: (_mod_row(i, ROW_TILE), 0, 0)),
            pl.BlockSpec((1, D_MODEL), const),
            wspec, wspec, wspec,
            pl.BlockSpec(conv_w.shape, const),
            wspec,
        ],
        out_specs=pl.BlockSpec((ROW_TILE, D_MODEL), row),
        scratch_shapes=[pltpu.VMEM((ROW_TILE + 2 * HALO, D_MODEL), F32)],
        compiler_params=_params(("parallel",)),
        name="shortconv_mixer",
    )(h, mods_l, g.reshape(1, D_MODEL), wb, wc, wh, conv_w, _mxu_weight(w_out))


def _gmlp_kernel(h_ref, mod_ref, g_ref, wu_ref, wv_ref, vn_ref, ws_ref, bs_ref, wo_ref, o_ref, a_ref):
    h = h_ref[...]
    u = _norm_mod(h, g_ref[...], mod_ref[3:4, :], mod_ref[4:5, :]).astype(BF16)
    zu = jax.nn.gelu(_dot(u, wu_ref[:, 0:GM_DIM]), approximate=True)
    zv = jax.nn.gelu(_dot(u, wv_ref[:, 0:GM_DIM]), approximate=True)
    zc = zv - jnp.mean(zv, axis=-1, keepdims=True)
    zv = zc * lax.rsqrt(jnp.mean(zc * zc, axis=-1, keepdims=True) + EPS) * vn_ref[...]
    zv = zv.astype(BF16)
    for c in range(ROW_TILE // GM_CHUNK):
        rs = slice(c * GM_CHUNK, (c + 1) * GM_CHUNK)
        for g in range(GM_GROUPS):
            gs = slice(g * GM_GROUP_DIM, (g + 1) * GM_GROUP_DIM)
            s = _dot(ws_ref[g], zv[rs, gs]) + bs_ref[:, g:g + 1]
            a_ref[rs, gs] = (zu[rs, gs] * s).astype(BF16)
    o_ref[...] = h + mod_ref[5:6, :] * _dot(a_ref[...], wo_ref[:, 0:D_MODEL])


def _gmlp_mixer(h, mods_l, g, w_in, v_norm, w_s, b_s, w_out):
    wu = _mxu_weight(w_in[:, :GM_DIM])
    wv = _mxu_weight(w_in[:, GM_DIM:])
    wo = _mxu_weight(w_out)
    row = lambda i: (i, 0)
    const = lambda i: (0, 0)
    return pl.pallas_call(
        _gmlp_kernel,
        out_shape=jax.ShapeDtypeStruct((N_TOK, D_MODEL), F32),
        grid=(N_TOK // ROW_TILE,),
        in_specs=[
            pl.BlockSpec((ROW_TILE, D_MODEL), row),
            pl.BlockSpec((None, N_MOD, D_MODEL), lambda i: (_mod_row(i, ROW_TILE), 0, 0)),
            pl.BlockSpec((1, D_MODEL), const),
            _resident(wu.shape),
            _resident(wv.shape),
            pl.BlockSpec((1, GM_DIM), const),
            pl.BlockSpec(w_s.shape, lambda i: (0, 0, 0)),
            pl.BlockSpec((GM_CHUNK, GM_GROUPS), const),
            _resident(wo.shape),
        ],
        out_specs=pl.BlockSpec((ROW_TILE, D_MODEL), row),
        scratch_shapes=[pltpu.VMEM((ROW_TILE, GM_DIM), BF16)],
        compiler_params=_params(("parallel",)),
        name="gmlp_mixer",
    )(h, mods_l, g.reshape(1, D_MODEL), wu, wv, v_norm.reshape(1, GM_DIM), w_s.astype(BF16),
      b_s.T, wo)


def kernel(x, c, ctx, c_ctx, ada_w, ada_b, norm_g, ffn_wg, ffn_wu, ffn_wd, ssd_in, ssd_conv_w, ssd_conv_b, ssd_dt_bias, ssd_a_log, ssd_d, ssd_norm, ssd_out, sc_in, sc_conv, sc_out, gm_in, gm_vnorm, gm_ws, gm_bs, gm_out, final_norm):
    h = jnp.concatenate([x.reshape(N_LAT, D_MODEL), ctx.reshape(N_CTX, D_MODEL)], axis=0)
    mods = _modulations(c, c_ctx, ada_w, ada_b)
    for i in range(DEPTH):
        last = i == DEPTH - 1
        m = mods[i]
        h = _half_ffn(h, m, norm_g[i, 0], _mxu_weight(ffn_wg[i, 0]), _mxu_weight(ffn_wu[i, 0]),
                      _mxu_weight(ffn_wd[i, 0]), 0, N_TOK)
        kind, j = i % 3, i // 3
        if kind == 0:
            h = _ssd_mixer(h, m, norm_g[i, 1], ssd_in[j], ssd_conv_w[j], ssd_conv_b[j], ssd_dt_bias[j],
                           ssd_a_log[j], ssd_d[j], ssd_norm[j], ssd_out[j])
        elif kind == 1:
            h = _shortconv_mixer(h, m, norm_g[i, 1], sc_in[j], sc_conv[j], sc_out[j])
        else:
            h = _gmlp_mixer(h, m, norm_g[i, 1], gm_in[j], gm_vnorm[j], gm_ws[j], gm_bs[j], gm_out[j])
        h = _half_ffn(h, m, norm_g[i, 2], _mxu_weight(ffn_wg[i, 1]), _mxu_weight(ffn_wu[i, 1]),
                      _mxu_weight(ffn_wd[i, 1]), 6, N_LAT if last else N_TOK,
                      final_g=final_norm if last else None)
    return h.reshape(BATCH, SEQ, D_MODEL)
```

```python
import functools

import jax
import jax.numpy as jnp
import numpy as np
from jax import lax
from jax.experimental import pallas as pl
from jax.experimental.pallas import tpu as pltpu

F32 = jnp.float32
BF16 = jnp.bfloat16

D_MODEL = 1024
BATCH = 2
SEQ = 8192
DEPTH = 4
GRID_W = 64
CTX_LEN = 256
N_MOD = 9
EPS = 1e-6
LOG2E = 1.4426950408889634
FFN_DIM = 2816

SSD_INNER = 2048
SSD_HEADDIM = 64
SSD_HEADS = 32
SSD_STATE = 128
SSD_GROUPS = 4
SSD_HPG = 8
SSD_CONV = 5
SSD_CHUNK = 128
SSD_BC = SSD_GROUPS * SSD_STATE
SSD_CONV_DIM = SSD_INNER + 2 * SSD_BC
SSD_GROUP_W = SSD_HPG * SSD_HEADDIM

GM_CHUNK = 128
GM_DIM = 2048
GM_GROUPS = 8
GM_GROUP_DIM = 256

N_LAT = BATCH * SEQ
N_CTX = BATCH * CTX_LEN
N_TOK = N_LAT + N_CTX
CTX_MOD_ROW = BATCH

LANE = 128
SUBLANE = 8
VMEM_LIMIT = 56 * 1024 * 1024

FFN_TM = 512
FFN_FC = 256
ROW_TILE = 256
HALO = SUBLANE
CONV_COLS = 256
SCAN_CHUNKS = 2
CONV_GROUPS = 8


def _dot(a, b):
    return jnp.dot(a, b, preferred_element_type=F32)


def _dot_nt(a, b):
    return lax.dot_general(a, b, (((1,), (1,)), ((), ())), preferred_element_type=F32)


def _mxu_weight(w):
    tiles = -(-w.shape[1] // LANE)
    pad = (tiles + (tiles + 1) % 2) * LANE - w.shape[1]
    return jnp.pad(w.astype(BF16), ((0, 0), (0, pad)))


def _silu(x):
    hx = 0.5 * x
    return hx + hx * jnp.tanh(hx)


def _mod_row(tile, tile_rows):
    return jnp.minimum((tile * tile_rows) // SEQ, CTX_MOD_ROW)


def _norm_mod(h, g, shift, scale):
    y = h * lax.rsqrt(jnp.mean(h * h, axis=-1, keepdims=True) + EPS) * g
    return y * (1.0 + scale) + shift


def _params(sem):
    return pltpu.CompilerParams(dimension_semantics=sem, vmem_limit_bytes=VMEM_LIMIT)


def _mod_kernel(ct_ref, w_ref, b_ref, o_ref, s_ref):
    n_vec = BATCH + 1
    tn = w_ref.shape[-1]

    @pl.when(jnp.logical_and(pl.program_id(0) == 0, pl.program_id(1) == 0))
    def _():
        s = _silu(ct_ref[...])
        for r in range(n_vec):
            s_ref[r] = jnp.broadcast_to(s[:, r:r + 1], (D_MODEL, LANE))

    def body(k, accs):
        r0 = pl.multiple_of(k * SUBLANE, SUBLANE)
        w8 = w_ref[pl.ds(r0, SUBLANE), :]
        out = []
        for r, a in enumerate(accs):
            s8 = s_ref[r, pl.ds(r0, SUBLANE), :]
            out.append(a + w8 * jnp.concatenate([s8] * (tn // LANE), axis=1))
        return tuple(out)

    zero = jnp.zeros((SUBLANE, tn), F32)
    accs = lax.fori_loop(0, D_MODEL // SUBLANE, body, (zero,) * n_vec, unroll=4)
    o_ref[...] = jnp.zeros(o_ref.shape, F32)
    for r, a in enumerate(accs):
        o_ref[r:r + 1, :] = jnp.sum(a, axis=0, keepdims=True) + b_ref[...]


def _modulations(c, c_ctx, ada_w, ada_b):
    ct = jnp.concatenate([c, c_ctx[None, :], jnp.zeros((SUBLANE - BATCH - 1, D_MODEL), F32)], axis=0).T
    tn = 1024
    n_out = N_MOD * D_MODEL
    out = pl.pallas_call(
        _mod_kernel,
        out_shape=jax.ShapeDtypeStruct((DEPTH, SUBLANE, n_out), F32),
        grid=(DEPTH, n_out // tn),
        in_specs=[
            pl.BlockSpec((D_MODEL, SUBLANE), lambda l, j: (0, 0)),
            pl.BlockSpec((None, D_MODEL, tn), lambda l, j: (l, 0, j)),
            pl.BlockSpec((None, 1, tn), lambda l, j: (l, 0, j)),
        ],
        out_specs=pl.BlockSpec((None, SUBLANE, tn), lambda l, j: (l, 0, j)),
        scratch_shapes=[pltpu.VMEM((BATCH + 1, D_MODEL, LANE), F32)],
        compiler_params=_params(("arbitrary", "arbitrary")),
        name="adaln_mod",
    )(ct, ada_w, ada_b.reshape(DEPTH, 1, n_out))
    return out.reshape(DEPTH, SUBLANE, N_MOD, D_MODEL)


def _ffn_kernel(h_ref, mod_ref, g_ref, wg_ref, wu_ref, wd_ref, *rest, mod_base, final):
    if final:
        fg_ref, o_ref, a_ref = rest
    else:
        o_ref, a_ref = rest
    h = h_ref[...]
    u = _norm_mod(h, g_ref[...], mod_ref[mod_base:mod_base + 1, :],
                  mod_ref[mod_base + 1:mod_base + 2, :]).astype(BF16)
    for c0 in range(0, FFN_DIM, FFN_FC):
        cs = slice(c0, c0 + FFN_FC)
        a_ref[:, cs] = (_silu(_dot(u, wg_ref[:, cs])) * _dot(u, wu_ref[:, cs])).astype(BF16)
    out = h + 0.5 * mod_ref[mod_base + 2:mod_base + 3, :] * _dot(a_ref[...], wd_ref[:, 0:D_MODEL])
    if final:
        out = out * lax.rsqrt(jnp.mean(out * out, axis=-1, keepdims=True) + EPS) * fg_ref[...]
    o_ref[...] = out


def _resident(shape):
    return pl.BlockSpec(shape, lambda *_: (0,) * len(shape), pipeline_mode=pl.Buffered(1))


def _half_ffn(h, mods_l, g, wg, wu, wd, mod_base, n_rows, final_g=None):
    final = final_g is not None
    row = lambda i: (i, 0)
    in_specs = [
        pl.BlockSpec((FFN_TM, D_MODEL), row),
        pl.BlockSpec((None, N_MOD, D_MODEL), lambda i: (_mod_row(i, FFN_TM), 0, 0)),
        _resident((1, D_MODEL)),
        _resident(wg.shape), _resident(wu.shape), _resident(wd.shape),
    ]
    args = [h, mods_l, g.reshape(1, D_MODEL), wg, wu, wd]
    if final:
        in_specs.append(_resident((1, D_MODEL)))
        args.append(final_g.reshape(1, D_MODEL))
    return pl.pallas_call(
        functools.partial(_ffn_kernel, mod_base=mod_base, final=final),
        out_shape=jax.ShapeDtypeStruct((n_rows, D_MODEL), F32),
        grid=(n_rows // FFN_TM,),
        in_specs=in_specs,
        out_specs=pl.BlockSpec((FFN_TM, D_MODEL), row),
        scratch_shapes=[pltpu.VMEM((FFN_TM, FFN_DIM), BF16)],
        compiler_params=_params(("parallel",)),
        name="half_ffn_final" if final else "half_ffn",
    )(*args)


def _out_proj_kernel(h_ref, mod_ref, a_ref, w_ref, o_ref):
    o_ref[...] = h_ref[...] + mod_ref[5:6, :] * _dot(a_ref[...], w_ref[:, 0:D_MODEL])


def _out_proj(h, mods_l, a, w, name):
    tm = 512
    row = lambda i: (i, 0)
    return pl.pallas_call(
        _out_proj_kernel,
        out_shape=jax.ShapeDtypeStruct((N_TOK, D_MODEL), F32),
        grid=(N_TOK // tm,),
        in_specs=[
            pl.BlockSpec((tm, D_MODEL), row),
            pl.BlockSpec((None, N_MOD, D_MODEL), lambda i: (_mod_row(i, tm), 0, 0)),
            pl.BlockSpec((tm, a.shape[1]), row),
            _resident(w.shape),
        ],
        out_specs=pl.BlockSpec((tm, D_MODEL), row),
        compiler_params=_params(("parallel",)),
        name=name,
    )(h, mods_l, a, w)


def _seq_edges(tile):
    lat_tiles = SEQ // ROW_TILE
    is_ctx = tile >= BATCH * lat_tiles
    pos = tile % lat_tiles
    return jnp.logical_or(is_ctx, pos == 0), jnp.logical_or(is_ctx, pos == lat_tiles - 1)


def _ssd_in_kernel(hp_ref, h_ref, hn_ref, mod_ref, g_ref, wz_ref, wx_ref, wdt_ref, cw_ref, cb_ref,
                   z_ref, x_ref, dt_ref):
    first, last = _seq_edges(pl.program_id(0))
    g, shift, scale = g_ref[...], mod_ref[3:4, :], mod_ref[4:5, :]
    u = _norm_mod(h_ref[...], g, shift, scale)
    u_ext = jnp.concatenate([_norm_mod(hp_ref[...], g, shift, scale), u,
                             _norm_mod(hn_ref[...], g, shift, scale)], axis=0).astype(BF16)
    ub = u.astype(BF16)
    n_grp = (ROW_TILE + 2 * HALO) // SUBLANE
    half = SSD_CONV // 2
    sub = lax.broadcasted_iota(jnp.int32, (1, SUBLANE, 1), 1)

    def project(blk):
        cols = slice(blk * CONV_COLS, (blk + 1) * CONV_COLS)
        e = _dot(u_ext, wx_ref[:, cols])
        e = jnp.concatenate([jnp.where(first, 0.0, e[0:HALO, :]), e[HALO:HALO + ROW_TILE, :],
                             jnp.where(last, 0.0, e[HALO + ROW_TILE:, :])], axis=0)
        return e.reshape(n_grp, SUBLANE, CONV_COLS)

    def conv(blk, e3):
        cols = slice(blk * CONV_COLS, (blk + 1) * CONV_COLS)
        for g0 in range(0, ROW_TILE // SUBLANE, CONV_GROUPS):
            ein = e3[g0:g0 + CONV_GROUPS + 2]
            acc = cb_ref[:, cols] + cw_ref[half:half + 1, cols] * ein[1:CONV_GROUPS + 1]
            for k in range(SSD_CONV):
                s = k - half
                if s == 0:
                    continue
                r3 = pltpu.roll(ein, (-s) % SUBLANE, axis=1)
                if s > 0:
                    shifted = jnp.where(sub < SUBLANE - s, r3[1:CONV_GROUPS + 1], r3[2:CONV_GROUPS + 2])
                else:
                    shifted = jnp.where(sub >= -s, r3[1:CONV_GROUPS + 1], r3[0:CONV_GROUPS])
                acc = acc + cw_ref[k:k + 1, cols] * shifted
            rows = slice(g0 * SUBLANE, (g0 + CONV_GROUPS) * SUBLANE)
            x_ref[rows, cols] = _silu(acc).reshape(CONV_GROUPS * SUBLANE, CONV_COLS)

    n_x = SSD_CONV_DIM // CONV_COLS
    n_z = SSD_INNER // CONV_COLS
    e3 = project(0)
    for blk in range(n_x):
        e3_next = project(blk + 1) if blk + 1 < n_x else None
        if blk < n_z:
            cols = slice(blk * CONV_COLS, (blk + 1) * CONV_COLS)
            z_ref[:, cols] = _dot(ub, wz_ref[:, cols])
        elif blk == n_z:
            dt_ref[...] = _dot(ub, wdt_ref[...])
        conv(blk, e3)
        e3 = e3_next


def _ssd_in(h, mods_l, g, wz, wx, wdt, conv_w, conv_b):
    per = ROW_TILE // HALO
    n_halo = N_TOK // HALO
    row = lambda i: (i, 0)
    widths = (SSD_INNER, SSD_CONV_DIM, LANE)
    return pl.pallas_call(
        _ssd_in_kernel,
        out_shape=[jax.ShapeDtypeStruct((N_TOK, n), F32) for n in widths],
        grid=(N_TOK // ROW_TILE,),
        in_specs=[
            pl.BlockSpec((HALO, D_MODEL), lambda i: (jnp.maximum(i * per - 1, 0), 0)),
            pl.BlockSpec((ROW_TILE, D_MODEL), row),
            pl.BlockSpec((HALO, D_MODEL), lambda i: (jnp.minimum((i + 1) * per, n_halo - 1), 0)),
            pl.BlockSpec((None, N_MOD, D_MODEL), lambda i: (_mod_row(i, ROW_TILE), 0, 0)),
            _resident((1, D_MODEL)),
            _resident(wz.shape), _resident(wx.shape), _resident(wdt.shape),
            _resident(conv_w.shape), _resident((1, SSD_CONV_DIM)),
        ],
        out_specs=[pl.BlockSpec((ROW_TILE, n), row) for n in widths],
        compiler_params=_params(("parallel",)),
        name="ssd_in_proj",
    )(h, h, h, mods_l, g.reshape(1, D_MODEL), wz, wx, wdt, conv_w, conv_b.reshape(1, SSD_CONV_DIM))


def _split_hi_lo(x):
    hi = x.astype(BF16)
    lo = (x - hi.astype(F32)).astype(BF16)
    return jnp.concatenate([hi, lo], axis=1)


def _ssd_prep_kernel(dt_ref, bias_ref, alog_ref, cs2_ref, rowt_ref, ws_ref):
    q = SSD_CHUNK
    ii = lax.broadcasted_iota(jnp.int32, (q, q), 0)
    jj = lax.broadcasted_iota(jnp.int32, (q, q), 1)
    lower = (jj <= ii).astype(F32)
    upper = (jj >= ii).astype(F32)
    is_fwd = lax.broadcasted_iota(jnp.int32, (q, LANE), 1) < SSD_HEADS
    neg_a = -jnp.exp(alog_ref[...])
    for c in range(ROW_TILE // q):
        rows = slice(c * q, (c + 1) * q)
        x_raw = dt_ref[rows, :] + bias_ref[...]
        dt = jnp.maximum(x_raw, 0.0) + jnp.log1p(jnp.exp(-jnp.abs(x_raw)))
        da = dt * neg_a
        cs_f = jnp.dot(lower, da, precision=lax.Precision.HIGHEST, preferred_element_type=F32)
        cs_b = jnp.dot(upper, da, precision=lax.Precision.HIGHEST, preferred_element_type=F32)
        cs = jnp.where(is_fwd, cs_f, cs_b)
        total = jnp.where(is_fwd[0:1, :], cs_f[q - 1:q, :], cs_b[0:1, :])
        cs2 = cs * LOG2E
        cs2_ref[rows, :] = cs2
        rowt_ref[rows, :] = (cs2 - jnp.log2(dt)).T
        w_end = dt * jnp.exp(total - cs)
        chunk_decay = jnp.broadcast_to(jnp.exp(total), (2 * SUBLANE, LANE))
        ws_ref[c] = _split_hi_lo(jnp.concatenate([w_end, chunk_decay], axis=0))


def _ssd_prep(dt_raw, bias_pad, alog_pad):
    q = SSD_CHUNK
    per = ROW_TILE // q
    row = lambda i: (i, 0)
    const = lambda i: (0, 0)
    return pl.pallas_call(
        _ssd_prep_kernel,
        out_shape=[jax.ShapeDtypeStruct((N_TOK, LANE), F32), jax.ShapeDtypeStruct((N_TOK, LANE), F32),
                   jax.ShapeDtypeStruct((N_TOK // q, q + 2 * SUBLANE, 2 * LANE), BF16)],
        grid=(N_TOK // ROW_TILE,),
        in_specs=[pl.BlockSpec((ROW_TILE, LANE), row), pl.BlockSpec((1, LANE), const),
                  pl.BlockSpec((1, LANE), const)],
        out_specs=[pl.BlockSpec((ROW_TILE, LANE), row), pl.BlockSpec((ROW_TILE, LANE), row),
                   pl.BlockSpec((per, q + 2 * SUBLANE, 2 * LANE), lambda i: (i, 0, 0))],
        compiler_params=_params(("parallel",)),
        name="ssd_prep",
    )(dt_raw, bias_pad, alog_pad)


def _ssd_scan_kernel(xbc_ref, cs2_ref, rowt_ref, ws_ref, e2_ref, *rest, reverse):
    if reverse:
        yf_ref, z_ref, dexp_ref, nw_ref, o_ref, st_ref, y_ref = rest
    else:
        o_ref, st_ref = rest
        y_ref = o_ref
    q = SSD_CHUNK
    off = SSD_HEADS if reverse else 0

    @pl.when(pl.program_id(1) == 0)
    def _():
        st_ref[...] = jnp.zeros(st_ref.shape, F32)

    ii = lax.broadcasted_iota(jnp.int32, (q, q), 0)
    jj = lax.broadcasted_iota(jnp.int32, (q, q), 1)
    tri = (jj >= ii) if reverse else (jj <= ii)
    lane = lax.broadcasted_iota(jnp.int32, (q, LANE), 1)

    def chunk(c):
        r0 = c * q
        rows = slice(r0, r0 + q)
        cs2 = cs2_ref[rows, :]
        spread = _dot(ws_ref[c], e2_ref[:, 0:SSD_INNER])
        w_exp = spread[0:q, :]
        decay_exp = spread[q:q + 1, :]

        for g in range(SSD_GROUPS):
            b_g = xbc_ref[rows, SSD_INNER + g * SSD_STATE:SSD_INNER + (g + 1) * SSD_STATE]
            c_lo = SSD_INNER + SSD_BC + g * SSD_STATE
            c_g = xbc_ref[rows, c_lo:c_lo + SSD_STATE]
            cb = _dot_nt(c_g.astype(BF16), b_g.astype(BF16))
            gs = slice(g * SSD_GROUP_W, (g + 1) * SSD_GROUP_W)
            st_old = st_ref[:, gs]
            st_bf = st_old.astype(BF16)
            x_g = xbc_ref[rows, gs]
            for k in range(SSD_GROUP_W // LANE):
                ls = slice(k * LANE, (k + 1) * LANE)
                rhs = jnp.concatenate([x_g[:, ls].astype(BF16), st_bf[:, ls]], axis=0)
                lhs = []
                for h in (g * SSD_HPG + 2 * k, g * SSD_HPG + 2 * k + 1):
                    col = off + h
                    cs_i = jnp.broadcast_to(cs2[:, col:col + 1], (q, q))
                    row_j = rowt_ref[r0 + col:r0 + col + 1, :]
                    m = jnp.exp2(jnp.where(tri, cs_i - row_j, -jnp.inf)) * cb
                    c_scaled = c_g * jnp.exp2(cs_i)
                    lhs.append(jnp.concatenate([m.astype(BF16), c_scaled.astype(BF16)], axis=1))
                out = _dot(jnp.concatenate(lhs, axis=0), rhs)
                y_pair = jnp.where(lane < SSD_HEADDIM, out[0:q, :], out[q:2 * q, :])
                cols = slice(g * SSD_GROUP_W + k * LANE, g * SSD_GROUP_W + (k + 1) * LANE)
                if reverse:
                    y_pair = y_pair + yf_ref[rows, cols] + dexp_ref[:, cols] * x_g[:, ls]
                    y_pair = y_pair * _silu(z_ref[rows, cols])
                y_ref[rows, cols] = y_pair
            xw = (x_g * w_exp[:, gs]).astype(BF16)
            st_ref[:, gs] = st_old * decay_exp[:, gs] + _dot(b_g.T.astype(BF16), xw)

        if reverse:
            for g in range(SSD_GROUPS):
                gs = slice(g * SSD_GROUP_W, (g + 1) * SSD_GROUP_W)
                y = y_ref[rows, gs]
                y = y * lax.rsqrt(jnp.mean(y * y, axis=-1, keepdims=True) + EPS) * nw_ref[:, gs]
                o_ref[rows, gs] = y.astype(o_ref.dtype)

    order = range(SCAN_CHUNKS - 1, -1, -1) if reverse else range(SCAN_CHUNKS)
    for c in order:
        chunk(c)


def _ssd_scan(xbc_act, cs2, rowt, ws, e2, reverse, extra=()):
    block = SCAN_CHUNKS * SSD_CHUNK
    lat_steps = SEQ // block
    ctx_base = N_LAT // block
    steps = 1 + lat_steps

    def blk(b, s):
        lat = (lat_steps - s) if reverse else (s - 1)
        return jnp.where(s == 0, ctx_base + b, b * lat_steps + lat)

    rows = lambda w: pl.BlockSpec((block, w), lambda b, s: (blk(b, s), 0))
    const = lambda shape: pl.BlockSpec(shape, lambda b, s: (0, 0))
    ws_spec = pl.BlockSpec((SCAN_CHUNKS,) + ws.shape[1:], lambda b, s: (blk(b, s), 0, 0))
    in_specs = [rows(SSD_CONV_DIM), rows(LANE), rows(LANE), ws_spec, const(e2.shape)]
    args = [xbc_act, cs2, rowt, ws, e2]
    if reverse:
        y_f, z, d_exp, norm_w = extra
        in_specs += [rows(SSD_INNER), rows(SSD_INNER), const((1, SSD_INNER)), const((1, SSD_INNER))]
        args += [y_f, z, d_exp, norm_w]
    scratch = [pltpu.VMEM((SSD_STATE, SSD_INNER), F32)]
    if reverse:
        scratch.append(pltpu.VMEM((block, SSD_INNER), F32))
    return pl.pallas_call(
        functools.partial(_ssd_scan_kernel, reverse=reverse),
        out_shape=jax.ShapeDtypeStruct((N_TOK, SSD_INNER), BF16 if reverse else F32),
        grid=(BATCH, steps),
        in_specs=in_specs,
        out_specs=rows(SSD_INNER),
        scratch_shapes=scratch,
        compiler_params=_params(("parallel", "arbitrary")),
        name="ssd_scan_bwd" if reverse else "ssd_scan_fwd",
    )(*args)


def _head_spread_matrix(off):
    e = np.zeros((LANE, SSD_INNER), np.float32)
    for h in range(SSD_HEADS):
        e[off + h, h * SSD_HEADDIM:(h + 1) * SSD_HEADDIM] = 1.0
    return _mxu_weight(jnp.asarray(np.concatenate([e, e], axis=0)))


def _ssd_mixer(h, mods_l, g, w_in, conv_w, conv_b, dt_bias, a_log, d_skip, norm_w, w_out):
    wz = _mxu_weight(w_in[:, :SSD_INNER])
    wxbc = _mxu_weight(w_in[:, SSD_INNER:SSD_INNER + SSD_CONV_DIM])
    n_dt = 2 * SSD_HEADS
    wdt = _mxu_weight(w_in[:, SSD_INNER + SSD_CONV_DIM:])
    z, xbc_act, dt_raw = _ssd_in(h, mods_l, g, wz, wxbc, wdt, conv_w, conv_b)
    bias_pad = jnp.pad(dt_bias.reshape(1, n_dt), ((0, 0), (0, LANE - n_dt)))
    alog_pad = jnp.pad(a_log.reshape(1, n_dt), ((0, 0), (0, LANE - n_dt)))
    cs2, rowt, ws = _ssd_prep(dt_raw, bias_pad, alog_pad)
    y_f = _ssd_scan(xbc_act, cs2, rowt, ws, _head_spread_matrix(0), False)
    d_exp = jnp.repeat(d_skip, SSD_HEADDIM).reshape(1, SSD_INNER)
    y = _ssd_scan(xbc_act, cs2, rowt, ws, _head_spread_matrix(SSD_HEADS), True,
                  extra=(y_f, z, d_exp, norm_w.reshape(1, SSD_INNER)))
    return _out_proj(h, mods_l, y, w_out.astype(BF16), "ssd_out_proj")


def _shortconv_kernel(h_ref, mod_ref, g_ref, wb_ref, wc_ref, wh_ref, cw_ref, wo_ref, o_ref, v_ref):
    h = h_ref[...]
    u = _norm_mod(h, g_ref[...], mod_ref[3:4, :], mod_ref[4:5, :]).astype(BF16)
    v = _dot(u, wc_ref[:, 0:D_MODEL]) * _dot(u, wh_ref[:, 0:D_MODEL])
    v_ref[0:HALO, :] = jnp.zeros((HALO, D_MODEL), F32)
    v_ref[HALO:HALO + ROW_TILE, :] = v
    v_ref[HALO + ROW_TILE:, :] = jnp.zeros((HALO, D_MODEL), F32)
    is_ctx = pl.program_id(0) >= N_LAT // ROW_TILE
    row_len = jnp.where(is_ctx, CTX_LEN, GRID_W)
    pos = lax.broadcasted_iota(jnp.int32, (ROW_TILE, 1), 0) & (row_len - 1)
    v_prev = jnp.where(pos == 0, 0.0, v_ref[HALO - 1:HALO - 1 + ROW_TILE, :])
    v_next = jnp.where(pos == row_len - 1, 0.0, v_ref[HALO + 1:HALO + 1 + ROW_TILE, :])
    conv = cw_ref[0:1, :] * v_prev + cw_ref[1:2, :] * v + cw_ref[2:3, :] * v_next
    y = _dot((_dot(u, wb_ref[:, 0:D_MODEL]) * conv).astype(BF16), wo_ref[:, 0:D_MODEL])
    o_ref[...] = h + mod_ref[5:6, :] * y


def _shortconv_mixer(h, mods_l, g, w_in, conv_w, w_out):
    wb, wc, wh = (w_in[:, k * D_MODEL:(k + 1) * D_MODEL].astype(BF16) for k in range(3))
    row = lambda i: (i, 0)
    const = lambda i: (0, 0)
    wspec = _resident(wb.shape)
    return pl.pallas_call(
        _shortconv_kernel,
        out_shape=jax.ShapeDtypeStruct((N_TOK, D_MODEL), F32),
        grid=(N_TOK // ROW_TILE,),
        in_specs=[
            pl.BlockSpec((ROW_TILE, D_MODEL), row),
            pl.BlockSpec((None, N_MOD, D_MODEL), lambda i: (_mod_row(i, ROW_TILE), 0, 0)),
            pl.BlockSpec((1, D_MODEL), const),
            wspec, wspec, wspec,
            pl.BlockSpec(conv_w.shape, const),
            wspec,
        ],
        out_specs=pl.BlockSpec((ROW_TILE, D_MODEL), row),
        scratch_shapes=[pltpu.VMEM((ROW_TILE + 2 * HALO, D_MODEL), F32)],
        compiler_params=_params(("parallel",)),
        name="shortconv_mixer",
    )(h, mods_l, g.reshape(1, D_MODEL), wb, wc, wh, conv_w, w_out.astype(BF16))


def _gmlp_kernel(h_ref, mod_ref, g_ref, wu_ref, wv_ref, vn_ref, ws_ref, bs_ref, wo_ref, o_ref, a_ref):
    h = h_ref[...]
    u = _norm_mod(h, g_ref[...], mod_ref[3:4, :], mod_ref[4:5, :]).astype(BF16)
    zu = jax.nn.gelu(_dot(u, wu_ref[:, 0:GM_DIM]), approximate=True)
    zv = jax.nn.gelu(_dot(u, wv_ref[:, 0:GM_DIM]), approximate=True)
    zc = zv - jnp.mean(zv, axis=-1, keepdims=True)
    zv = zc * lax.rsqrt(jnp.mean(zc * zc, axis=-1, keepdims=True) + EPS) * vn_ref[...]
    zv = zv.astype(BF16)
    for c in range(ROW_TILE // GM_CHUNK):
        rs = slice(c * GM_CHUNK, (c + 1) * GM_CHUNK)
        for g in range(GM_GROUPS):
            gs = slice(g * GM_GROUP_DIM, (g + 1) * GM_GROUP_DIM)
            s = _dot(ws_ref[g], zv[rs, gs]) + bs_ref[:, g:g + 1]
            a_ref[rs, gs] = (zu[rs, gs] * s).astype(BF16)
    o_ref[...] = h + mod_ref[5:6, :] * _dot(a_ref[...], wo_ref[:, 0:D_MODEL])


def _gmlp_mixer(h, mods_l, g, w_in, v_norm, w_s, b_s, w_out):
    wu = _mxu_weight(w_in[:, :GM_DIM])
    wv = _mxu_weight(w_in[:, GM_DIM:])
    wo = _mxu_weight(w_out)
    row = lambda i: (i, 0)
    const = lambda i: (0, 0)
    return pl.pallas_call(
        _gmlp_kernel,
        out_shape=jax.ShapeDtypeStruct((N_TOK, D_MODEL), F32),
        grid=(N_TOK // ROW_TILE,),
        in_specs=[
            pl.BlockSpec((ROW_TILE, D_MODEL), row),
            pl.BlockSpec((None, N_MOD, D_MODEL), lambda i: (_mod_row(i, ROW_TILE), 0, 0)),
            pl.BlockSpec((1, D_MODEL), const),
            _resident(wu.shape),
            _resident(wv.shape),
            pl.BlockSpec((1, GM_DIM), const),
            pl.BlockSpec(w_s.shape, lambda i: (0, 0, 0)),
            pl.BlockSpec((GM_CHUNK, GM_GROUPS), const),
            _resident(wo.shape),
        ],
        out_specs=pl.BlockSpec((ROW_TILE, D_MODEL), row),
        scratch_shapes=[pltpu.VMEM((ROW_TILE, GM_DIM), BF16)],
        compiler_params=_params(("parallel",)),
        name="gmlp_mixer",
    )(h, mods_l, g.reshape(1, D_MODEL), wu, wv, v_norm.reshape(1, GM_DIM), w_s.astype(BF16),
      b_s.T, wo)


def kernel(x, c, ctx, c_ctx, ada_w, ada_b, norm_g, ffn_wg, ffn_wu, ffn_wd, ssd_in, ssd_conv_w, ssd_conv_b, ssd_dt_bias, ssd_a_log, ssd_d, ssd_norm, ssd_out, sc_in, sc_conv, sc_out, gm_in, gm_vnorm, gm_ws, gm_bs, gm_out, final_norm):
    h = jnp.concatenate([x.reshape(N_LAT, D_MODEL), ctx.reshape(N_CTX, D_MODEL)], axis=0)
    mods = _modulations(c, c_ctx, ada_w, ada_b)
    for i in range(DEPTH):
        last = i == DEPTH - 1
        m = mods[i]
        h = _half_ffn(h, m, norm_g[i, 0], ffn_wg[i, 0].astype(BF16), ffn_wu[i, 0].astype(BF16),
                      ffn_wd[i, 0].astype(BF16), 0, N_TOK)
        kind, j = i % 3, i // 3
        if kind == 0:
            h = _ssd_mixer(h, m, norm_g[i, 1], ssd_in[j], ssd_conv_w[j], ssd_conv_b[j], ssd_dt_bias[j],
                           ssd_a_log[j], ssd_d[j], ssd_norm[j], ssd_out[j])
        elif kind == 1:
            h = _shortconv_mixer(h, m, norm_g[i, 1], sc_in[j], sc_conv[j], sc_out[j])
        else:
            h = _gmlp_mixer(h, m, norm_g[i, 1], gm_in[j], gm_vnorm[j], gm_ws[j], gm_bs[j], gm_out[j])
        h = _half_ffn(h, m, norm_g[i, 2], ffn_wg[i, 1].astype(BF16), ffn_wu[i, 1].astype(BF16),
                      ffn_wd[i, 1].astype(BF16), 6, N_LAT if last else N_TOK,
                      final_g=final_norm if last else None)
    return h.reshape(BATCH, SEQ, D_MODEL)
```

```python
import functools

import jax
import jax.numpy as jnp
import numpy as np
from jax import lax
from jax.experimental import pallas as pl
from jax.experimental.pallas import tpu as pltpu

F32 = jnp.float32
BF16 = jnp.bfloat16

D_MODEL = 1024
BATCH = 2
SEQ = 8192
DEPTH = 4
GRID_W = 64
CTX_LEN = 256
N_MOD = 9
EPS = 1e-6
LOG2E = 1.4426950408889634
FFN_DIM = 2816

SSD_INNER = 2048
SSD_HEADDIM = 64
SSD_HEADS = 32
SSD_STATE = 128
SSD_GROUPS = 4
SSD_HPG = 8
SSD_CONV = 5
SSD_CHUNK = 128
SSD_BC = SSD_GROUPS * SSD_STATE
SSD_CONV_DIM = SSD_INNER + 2 * SSD_BC
SSD_GROUP_W = SSD_HPG * SSD_HEADDIM

GM_CHUNK = 128
GM_DIM = 2048
GM_GROUPS = 8
GM_GROUP_DIM = 256

N_LAT = BATCH * SEQ
N_CTX = BATCH * CTX_LEN
N_TOK = N_LAT + N_CTX
CTX_MOD_ROW = BATCH

LANE = 128
SUBLANE = 8
VMEM_LIMIT = 56 * 1024 * 1024

FFN_TM = 512
FFN_FC = 256
ROW_TILE = 256
PREP_TILE = 512
HALO = SUBLANE
CONV_COLS = 256
SCAN_CHUNKS = 2
CONV_GROUPS = 8


def _dot(a, b):
    return jnp.dot(a, b, preferred_element_type=F32)


def _dot_nt(a, b):
    return lax.dot_general(a, b, (((1,), (1,)), ((), ())), preferred_element_type=F32)


def _mxu_weight(w):
    tiles = -(-w.shape[1] // LANE)
    pad = (tiles + (tiles + 1) % 2) * LANE - w.shape[1]
    return jnp.pad(w.astype(BF16), ((0, 0), (0, pad)))


def _silu(x):
    hx = 0.5 * x
    return hx + hx * jnp.tanh(hx)


def _mod_row(tile, tile_rows):
    return jnp.minimum((tile * tile_rows) // SEQ, CTX_MOD_ROW)


def _norm_mod(h, g, shift, scale):
    y = h * lax.rsqrt(jnp.mean(h * h, axis=-1, keepdims=True) + EPS) * g
    return y * (1.0 + scale) + shift


def _params(sem):
    return pltpu.CompilerParams(dimension_semantics=sem, vmem_limit_bytes=VMEM_LIMIT)


def _mod_kernel(ct_ref, w_ref, b_ref, o_ref, s_ref):
    n_vec = BATCH + 1
    tn = w_ref.shape[-1]

    @pl.when(jnp.logical_and(pl.program_id(0) == 0, pl.program_id(1) == 0))
    def _():
        s = _silu(ct_ref[...])
        for r in range(n_vec):
            s_ref[r] = jnp.broadcast_to(s[:, r:r + 1], (D_MODEL, LANE))

    def body(k, accs):
        r0 = pl.multiple_of(k * SUBLANE, SUBLANE)
        w8 = w_ref[pl.ds(r0, SUBLANE), :]
        out = []
        for r, a in enumerate(accs):
            s8 = s_ref[r, pl.ds(r0, SUBLANE), :]
            out.append(a + w8 * jnp.concatenate([s8] * (tn // LANE), axis=1))
        return tuple(out)

    zero = jnp.zeros((SUBLANE, tn), F32)
    accs = lax.fori_loop(0, D_MODEL // SUBLANE, body, (zero,) * n_vec, unroll=4)
    o_ref[...] = jnp.zeros(o_ref.shape, F32)
    for r, a in enumerate(accs):
        o_ref[r:r + 1, :] = jnp.sum(a, axis=0, keepdims=True) + b_ref[...]


def _modulations(c, c_ctx, ada_w, ada_b):
    ct = jnp.concatenate([c, c_ctx[None, :], jnp.zeros((SUBLANE - BATCH - 1, D_MODEL), F32)], axis=0).T
    tn = 1024
    n_out = N_MOD * D_MODEL
    out = pl.pallas_call(
        _mod_kernel,
        out_shape=jax.ShapeDtypeStruct((DEPTH, SUBLANE, n_out), F32),
        grid=(DEPTH, n_out // tn),
        in_specs=[
            pl.BlockSpec((D_MODEL, SUBLANE), lambda l, j: (0, 0)),
            pl.BlockSpec((None, D_MODEL, tn), lambda l, j: (l, 0, j)),
            pl.BlockSpec((None, 1, tn), lambda l, j: (l, 0, j)),
        ],
        out_specs=pl.BlockSpec((None, SUBLANE, tn), lambda l, j: (l, 0, j)),
        scratch_shapes=[pltpu.VMEM((BATCH + 1, D_MODEL, LANE), F32)],
        compiler_params=_params(("arbitrary", "arbitrary")),
        name="adaln_mod",
    )(ct, ada_w, ada_b.reshape(DEPTH, 1, n_out))
    return out.reshape(DEPTH, SUBLANE, N_MOD, D_MODEL)


def _ffn_kernel(*refs, mod_base, final, split_input):
    if split_input:
        x_ref, ctx_ref, *refs = refs
        is_ctx = pl.program_id(0) >= N_LAT // FFN_TM
        h = jnp.where(is_ctx, ctx_ref[...], x_ref[...])
    else:
        h_ref, *refs = refs
        h = h_ref[...]
    mod_ref, g_ref, wg_ref, wu_ref, wd_ref, *rest = refs
    if final:
        fg_ref, o_ref, a_ref = rest
    else:
        o_ref, a_ref = rest
    u = _norm_mod(h, g_ref[...], mod_ref[mod_base:mod_base + 1, :],
                  mod_ref[mod_base + 1:mod_base + 2, :]).astype(BF16)
    for c0 in range(0, FFN_DIM, FFN_FC):
        cs = slice(c0, c0 + FFN_FC)
        a_ref[:, cs] = (_silu(_dot(u, wg_ref[:, cs])) * _dot(u, wu_ref[:, cs])).astype(BF16)
    out = h + 0.5 * mod_ref[mod_base + 2:mod_base + 3, :] * _dot(a_ref[...], wd_ref[:, 0:D_MODEL])
    if final:
        out = out * lax.rsqrt(jnp.mean(out * out, axis=-1, keepdims=True) + EPS) * fg_ref[...]
    o_ref[...] = out


def _resident(shape):
    return pl.BlockSpec(shape, lambda *_: (0,) * len(shape), pipeline_mode=pl.Buffered(1))


def _half_ffn(h, mods_l, g, weights, layer, half, n_rows, final_g=None):
    final = final_g is not None
    split_input = isinstance(h, tuple)
    mod_base = 6 * half
    row = lambda i: (i, 0)
    if split_input:
        lat_tiles = N_LAT // FFN_TM
        h_specs = [pl.BlockSpec((FFN_TM, D_MODEL), lambda i: (jnp.minimum(i, lat_tiles - 1), 0)),
                   pl.BlockSpec((FFN_TM, D_MODEL), lambda i: (jnp.maximum(i - lat_tiles, 0), 0))]
        h_args = list(h)
    else:
        h_specs, h_args = [pl.BlockSpec((FFN_TM, D_MODEL), row)], [h]
    w_spec = lambda w: pl.BlockSpec((None, None) + w.shape[2:], lambda i: (layer, half, 0, 0),
                                    pipeline_mode=pl.Buffered(1))
    in_specs = h_specs + [
        pl.BlockSpec((None, N_MOD, D_MODEL), lambda i: (_mod_row(i, FFN_TM), 0, 0)),
        _resident((1, D_MODEL)),
    ] + [w_spec(w) for w in weights]
    args = h_args + [mods_l, g.reshape(1, D_MODEL)] + list(weights)
    if final:
        in_specs.append(_resident((1, D_MODEL)))
        args.append(final_g.reshape(1, D_MODEL))
    return pl.pallas_call(
        functools.partial(_ffn_kernel, mod_base=mod_base, final=final, split_input=split_input),
        out_shape=jax.ShapeDtypeStruct((n_rows, D_MODEL), F32),
        grid=(n_rows // FFN_TM,),
        in_specs=in_specs,
        out_specs=pl.BlockSpec((FFN_TM, D_MODEL), row),
        scratch_shapes=[pltpu.VMEM((FFN_TM, FFN_DIM), BF16)],
        compiler_params=_params(("parallel",)),
        name="half_ffn_final" if final else "half_ffn",
    )(*args)


def _out_proj_kernel(h_ref, mod_ref, a_ref, w_ref, o_ref):
    o_ref[...] = h_ref[...] + mod_ref[5:6, :] * _dot(a_ref[...], w_ref[:, 0:D_MODEL])


def _out_proj(h, mods_l, a, w, name):
    tm = 512
    row = lambda i: (i, 0)
    return pl.pallas_call(
        _out_proj_kernel,
        out_shape=jax.ShapeDtypeStruct((N_TOK, D_MODEL), F32),
        grid=(N_TOK // tm,),
        in_specs=[
            pl.BlockSpec((tm, D_MODEL), row),
            pl.BlockSpec((None, N_MOD, D_MODEL), lambda i: (_mod_row(i, tm), 0, 0)),
            pl.BlockSpec((tm, a.shape[1]), row),
            _resident(w.shape),
        ],
        out_specs=pl.BlockSpec((tm, D_MODEL), row),
        compiler_params=_params(("parallel",)),
        name=name,
    )(h, mods_l, a, w)


def _seq_edges(tile):
    lat_tiles = SEQ // ROW_TILE
    is_ctx = tile >= BATCH * lat_tiles
    pos = tile % lat_tiles
    return jnp.logical_or(is_ctx, pos == 0), jnp.logical_or(is_ctx, pos == lat_tiles - 1)


def _ssd_in_kernel(hp_ref, h_ref, hn_ref, mod_ref, g_ref, wz_ref, wx_ref, wdt_ref, cw_ref, cb_ref,
                   z_ref, x_ref, dt_ref):
    first, last = _seq_edges(pl.program_id(0))
    g, shift, scale = g_ref[...], mod_ref[3:4, :], mod_ref[4:5, :]
    u = _norm_mod(h_ref[...], g, shift, scale)
    u_ext = jnp.concatenate([_norm_mod(hp_ref[...], g, shift, scale), u,
                             _norm_mod(hn_ref[...], g, shift, scale)], axis=0).astype(BF16)
    ub = u.astype(BF16)
    n_grp = (ROW_TILE + 2 * HALO) // SUBLANE
    half = SSD_CONV // 2
    sub = lax.broadcasted_iota(jnp.int32, (1, SUBLANE, 1), 1)

    def project(blk):
        cols = slice(blk * CONV_COLS, (blk + 1) * CONV_COLS)
        e = _dot(u_ext, wx_ref[:, cols])
        e = jnp.concatenate([jnp.where(first, 0.0, e[0:HALO, :]), e[HALO:HALO + ROW_TILE, :],
                             jnp.where(last, 0.0, e[HALO + ROW_TILE:, :])], axis=0)
        return e.reshape(n_grp, SUBLANE, CONV_COLS)

    def conv(blk, e3):
        cols = slice(blk * CONV_COLS, (blk + 1) * CONV_COLS)
        for g0 in range(0, ROW_TILE // SUBLANE, CONV_GROUPS):
            ein = e3[g0:g0 + CONV_GROUPS + 2]
            acc = cb_ref[:, cols] + cw_ref[half:half + 1, cols] * ein[1:CONV_GROUPS + 1]
            for k in range(SSD_CONV):
                s = k - half
                if s == 0:
                    continue
                r3 = pltpu.roll(ein, (-s) % SUBLANE, axis=1)
                if s > 0:
                    shifted = jnp.where(sub < SUBLANE - s, r3[1:CONV_GROUPS + 1], r3[2:CONV_GROUPS + 2])
                else:
                    shifted = jnp.where(sub >= -s, r3[1:CONV_GROUPS + 1], r3[0:CONV_GROUPS])
                acc = acc + cw_ref[k:k + 1, cols] * shifted
            rows = slice(g0 * SUBLANE, (g0 + CONV_GROUPS) * SUBLANE)
            x_ref[rows, cols] = _silu(acc).reshape(CONV_GROUPS * SUBLANE, CONV_COLS)

    n_x = SSD_CONV_DIM // CONV_COLS
    n_z = SSD_INNER // CONV_COLS
    e3 = project(0)
    for blk in range(n_x):
        e3_next = project(blk + 1) if blk + 1 < n_x else None
        if blk < n_z:
            cols = slice(blk * CONV_COLS, (blk + 1) * CONV_COLS)
            z_ref[:, cols] = _dot(ub, wz_ref[:, cols])
        elif blk == n_z:
            dt_ref[...] = _dot(ub, wdt_ref[...])
        conv(blk, e3)
        e3 = e3_next


def _ssd_in(h, mods_l, g, wz, wx, wdt, conv_w, conv_b):
    per = ROW_TILE // HALO
    n_halo = N_TOK // HALO
    row = lambda i: (i, 0)
    widths = (SSD_INNER, SSD_CONV_DIM, LANE)
    return pl.pallas_call(
        _ssd_in_kernel,
        out_shape=[jax.ShapeDtypeStruct((N_TOK, n), F32) for n in widths],
        grid=(N_TOK // ROW_TILE,),
        in_specs=[
            pl.BlockSpec((HALO, D_MODEL), lambda i: (jnp.maximum(i * per - 1, 0), 0)),
            pl.BlockSpec((ROW_TILE, D_MODEL), row),
            pl.BlockSpec((HALO, D_MODEL), lambda i: (jnp.minimum((i + 1) * per, n_halo - 1), 0)),
            pl.BlockSpec((None, N_MOD, D_MODEL), lambda i: (_mod_row(i, ROW_TILE), 0, 0)),
            _resident((1, D_MODEL)),
            _resident(wz.shape), _resident(wx.shape), _resident(wdt.shape),
            _resident(conv_w.shape), _resident((1, SSD_CONV_DIM)),
        ],
        out_specs=[pl.BlockSpec((ROW_TILE, n), row) for n in widths],
        compiler_params=_params(("parallel",)),
        name="ssd_in_proj",
    )(h, h, h, mods_l, g.reshape(1, D_MODEL), wz, wx, wdt, conv_w, conv_b.reshape(1, SSD_CONV_DIM))


def _split_hi_lo(x):
    hi = x.astype(BF16)
    lo = (x - hi.astype(F32)).astype(BF16)
    return jnp.concatenate([hi, lo], axis=1)


def _ssd_prep_kernel(dt_ref, bias_ref, alog_ref, cs2_ref, rowt_ref, ws_ref):
    q = SSD_CHUNK
    ii = lax.broadcasted_iota(jnp.int32, (q, q), 0)
    jj = lax.broadcasted_iota(jnp.int32, (q, q), 1)
    lower = (jj <= ii).astype(F32)
    upper = (jj >= ii).astype(F32)
    is_fwd = lax.broadcasted_iota(jnp.int32, (q, LANE), 1) < SSD_HEADS
    neg_a = -jnp.exp(alog_ref[...])
    for c in range(PREP_TILE // q):
        rows = slice(c * q, (c + 1) * q)
        x_raw = dt_ref[rows, :] + bias_ref[...]
        dt = jnp.maximum(x_raw, 0.0) + jnp.log1p(jnp.exp(-jnp.abs(x_raw)))
        da = dt * neg_a
        cs_f = jnp.dot(lower, da, precision=lax.Precision.HIGHEST, preferred_element_type=F32)
        cs_b = jnp.dot(upper, da, precision=lax.Precision.HIGHEST, preferred_element_type=F32)
        cs = jnp.where(is_fwd, cs_f, cs_b)
        total = jnp.where(is_fwd[0:1, :], cs_f[q - 1:q, :], cs_b[0:1, :])
        cs2 = cs * LOG2E
        cs2_ref[rows, :] = cs2
        rowt_ref[rows, :] = (cs2 - jnp.log2(dt)).T
        w_end = dt * jnp.exp(total - cs)
        chunk_decay = jnp.broadcast_to(jnp.exp(total), (2 * SUBLANE, LANE))
        ws_ref[c] = _split_hi_lo(jnp.concatenate([w_end, chunk_decay], axis=0))


def _ssd_prep(dt_raw, bias_pad, alog_pad):
    q = SSD_CHUNK
    per = PREP_TILE // q
    row = lambda i: (i, 0)
    const = lambda i: (0, 0)
    return pl.pallas_call(
        _ssd_prep_kernel,
        out_shape=[jax.ShapeDtypeStruct((N_TOK, LANE), F32), jax.ShapeDtypeStruct((N_TOK, LANE), F32),
                   jax.ShapeDtypeStruct((N_TOK // q, q + 2 * SUBLANE, 2 * LANE), BF16)],
        grid=(N_TOK // PREP_TILE,),
        in_specs=[pl.BlockSpec((PREP_TILE, LANE), row), pl.BlockSpec((1, LANE), const),
                  pl.BlockSpec((1, LANE), const)],
        out_specs=[pl.BlockSpec((PREP_TILE, LANE), row), pl.BlockSpec((PREP_TILE, LANE), row),
                   pl.BlockSpec((per, q + 2 * SUBLANE, 2 * LANE), lambda i: (i, 0, 0))],
        compiler_params=_params(("parallel",)),
        name="ssd_prep",
    )(dt_raw, bias_pad, alog_pad)


def _ssd_scan_kernel(xbc_ref, cs2_ref, rowt_ref, ws_ref, e2_ref, *rest, reverse):
    if reverse:
        yf_ref, z_ref, dexp_ref, nw_ref, o_ref, st_ref, lhs_ref, y_ref = rest
    else:
        o_ref, st_ref, lhs_ref = rest
        y_ref = o_ref
    q = SSD_CHUNK
    off = SSD_HEADS if reverse else 0

    @pl.when(pl.program_id(1) == 0)
    def _():
        st_ref[...] = jnp.zeros(st_ref.shape, F32)

    ii = lax.broadcasted_iota(jnp.int32, (q, q), 0)
    jj = lax.broadcasted_iota(jnp.int32, (q, q), 1)
    tri = (jj >= ii) if reverse else (jj <= ii)
    lane = lax.broadcasted_iota(jnp.int32, (q, LANE), 1)

    pairs = SSD_GROUP_W // LANE

    def build(c, g):
        r0 = c * q
        rows = slice(r0, r0 + q)
        b_g = xbc_ref[rows, SSD_INNER + g * SSD_STATE:SSD_INNER + (g + 1) * SSD_STATE]
        c_lo = SSD_INNER + SSD_BC + g * SSD_STATE
        c_g = xbc_ref[rows, c_lo:c_lo + SSD_STATE]
        cb = _dot_nt(c_g.astype(BF16), b_g.astype(BF16))
        for k in range(pairs):
            buf = lhs_ref.at[c, g * pairs + k]
            for hh, h in enumerate((g * SSD_HPG + 2 * k, g * SSD_HPG + 2 * k + 1)):
                col = off + h
                cs_i = jnp.broadcast_to(cs2_ref[rows, col:col + 1], (q, q))
                row_j = rowt_ref[r0 + col:r0 + col + 1, :]
                m = jnp.exp2(jnp.where(tri, cs_i - row_j, -jnp.inf)) * cb
                c_scaled = c_g * jnp.exp2(cs_i)
                buf[hh * q:(hh + 1) * q, 0:q] = m.astype(BF16)
                buf[hh * q:(hh + 1) * q, q:2 * q] = c_scaled.astype(BF16)

    def apply(c, g, spread):
        rows = slice(c * q, (c + 1) * q)
        b_g = xbc_ref[rows, SSD_INNER + g * SSD_STATE:SSD_INNER + (g + 1) * SSD_STATE]
        gs = slice(g * SSD_GROUP_W, (g + 1) * SSD_GROUP_W)
        st_old = st_ref[:, gs]
        st_bf = st_old.astype(BF16)
        x_g = xbc_ref[rows, gs]
        for k in range(pairs):
            ls = slice(k * LANE, (k + 1) * LANE)
            rhs = jnp.concatenate([x_g[:, ls].astype(BF16), st_bf[:, ls]], axis=0)
            out = _dot(lhs_ref[c, g * pairs + k], rhs)
            y_pair = jnp.where(lane < SSD_HEADDIM, out[0:q, :], out[q:2 * q, :])
            cols = slice(g * SSD_GROUP_W + k * LANE, g * SSD_GROUP_W + (k + 1) * LANE)
            y_ref[rows, cols] = y_pair
        xw = (x_g * spread[0:q, gs]).astype(BF16)
        st_ref[:, gs] = st_old * spread[q:q + 1, gs] + _dot(b_g.T.astype(BF16), xw)

    def finish(c):
        if reverse:
            rows = slice(c * q, (c + 1) * q)
            for g in range(SSD_GROUPS):
                gs = slice(g * SSD_GROUP_W, (g + 1) * SSD_GROUP_W)
                y = y_ref[rows, gs] + yf_ref[rows, gs] + dexp_ref[:, gs] * xbc_ref[rows, gs]
                y = y * _silu(z_ref[rows, gs])
                y = y * lax.rsqrt(jnp.mean(y * y, axis=-1, keepdims=True) + EPS) * nw_ref[:, gs]
                o_ref[rows, gs] = y.astype(o_ref.dtype)

    order = list(range(SCAN_CHUNKS - 1, -1, -1) if reverse else range(SCAN_CHUNKS))
    for g in range(SSD_GROUPS):
        build(order[0], g)
    for n, c in enumerate(order):
        spread = _dot(ws_ref[c], e2_ref[:, 0:SSD_INNER])
        for g in range(SSD_GROUPS):
            if n + 1 < len(order):
                build(order[n + 1], g)
            apply(c, g, spread)
        finish(c)


def _ssd_scan(xbc_act, cs2, rowt, ws, e2, reverse, extra=()):
    block = SCAN_CHUNKS * SSD_CHUNK
    lat_steps = SEQ // block
    ctx_base = N_LAT // block
    steps = 1 + lat_steps

    def blk(b, s):
        lat = (lat_steps - s) if reverse else (s - 1)
        return jnp.where(s == 0, ctx_base + b, b * lat_steps + lat)

    rows = lambda w: pl.BlockSpec((block, w), lambda b, s: (blk(b, s), 0))
    const = lambda shape: pl.BlockSpec(shape, lambda b, s: (0, 0))
    ws_spec = pl.BlockSpec((SCAN_CHUNKS,) + ws.shape[1:], lambda b, s: (blk(b, s), 0, 0))
    in_specs = [rows(SSD_CONV_DIM), rows(LANE), rows(LANE), ws_spec, const(e2.shape)]
    args = [xbc_act, cs2, rowt, ws, e2]
    if reverse:
        y_f, z, d_exp, norm_w = extra
        in_specs += [rows(SSD_INNER), rows(SSD_INNER), const((1, SSD_INNER)), const((1, SSD_INNER))]
        args += [y_f, z, d_exp, norm_w]
    scratch = [pltpu.VMEM((SSD_STATE, SSD_INNER), F32),
               pltpu.VMEM((SCAN_CHUNKS, SSD_HEADS // 2, 2 * SSD_CHUNK, 2 * SSD_CHUNK), BF16)]
    if reverse:
        scratch.append(pltpu.VMEM((block, SSD_INNER), F32))
    return pl.pallas_call(
        functools.partial(_ssd_scan_kernel, reverse=reverse),
        out_shape=jax.ShapeDtypeStruct((N_TOK, SSD_INNER), BF16 if reverse else F32),
        grid=(BATCH, steps),
        in_specs=in_specs,
        out_specs=rows(SSD_INNER),
        scratch_shapes=scratch,
        compiler_params=_params(("parallel", "arbitrary")),
        name="ssd_scan_bwd" if reverse else "ssd_scan_fwd",
    )(*args)


def _head_spread_matrix(off):
    e = np.zeros((LANE, SSD_INNER), np.float32)
    for h in range(SSD_HEADS):
        e[off + h, h * SSD_HEADDIM:(h + 1) * SSD_HEADDIM] = 1.0
    return _mxu_weight(jnp.asarray(np.concatenate([e, e], axis=0)))


def _ssd_mixer(h, mods_l, g, w_in, conv_w, conv_b, dt_bias, a_log, d_skip, norm_w, w_out):
    wz = _mxu_weight(w_in[:, :SSD_INNER])
    wxbc = _mxu_weight(w_in[:, SSD_INNER:SSD_INNER + SSD_CONV_DIM])
    n_dt = 2 * SSD_HEADS
    wdt = _mxu_weight(w_in[:, SSD_INNER + SSD_CONV_DIM:])
    z, xbc_act, dt_raw = _ssd_in(h, mods_l, g, wz, wxbc, wdt, conv_w, conv_b)
    bias_pad = jnp.pad(dt_bias.reshape(1, n_dt), ((0, 0), (0, LANE - n_dt)))
    alog_pad = jnp.pad(a_log.reshape(1, n_dt), ((0, 0), (0, LANE - n_dt)))
    cs2, rowt, ws = _ssd_prep(dt_raw, bias_pad, alog_pad)
    y_f = _ssd_scan(xbc_act, cs2, rowt, ws, _head_spread_matrix(0), False)
    d_exp = jnp.repeat(d_skip, SSD_HEADDIM).reshape(1, SSD_INNER)
    y = _ssd_scan(xbc_act, cs2, rowt, ws, _head_spread_matrix(SSD_HEADS), True,
                  extra=(y_f, z, d_exp, norm_w.reshape(1, SSD_INNER)))
    return _out_proj(h, mods_l, y, w_out.astype(BF16), "ssd_out_proj")


def _shortconv_kernel(h_ref, mod_ref, g_ref, wb_ref, wc_ref, wh_ref, cw_ref, wo_ref, o_ref, v_ref):
    h = h_ref[...]
    u = _norm_mod(h, g_ref[...], mod_ref[3:4, :], mod_ref[4:5, :]).astype(BF16)
    v = _dot(u, wc_ref[:, 0:D_MODEL]) * _dot(u, wh_ref[:, 0:D_MODEL])
    v_ref[0:HALO, :] = jnp.zeros((HALO, D_MODEL), F32)
    v_ref[HALO:HALO + ROW_TILE, :] = v
    v_ref[HALO + ROW_TILE:, :] = jnp.zeros((HALO, D_MODEL), F32)
    is_ctx = pl.program_id(0) >= N_LAT // ROW_TILE
    row_len = jnp.where(is_ctx, CTX_LEN, GRID_W)
    pos = lax.broadcasted_iota(jnp.int32, (ROW_TILE, 1), 0) & (row_len - 1)
    v_prev = jnp.where(pos == 0, 0.0, v_ref[HALO - 1:HALO - 1 + ROW_TILE, :])
    v_next = jnp.where(pos == row_len - 1, 0.0, v_ref[HALO + 1:HALO + 1 + ROW_TILE, :])
    conv = cw_ref[0:1, :] * v_prev + cw_ref[1:2, :] * v + cw_ref[2:3, :] * v_next
    y = _dot((_dot(u, wb_ref[:, 0:D_MODEL]) * conv).astype(BF16), wo_ref[:, 0:D_MODEL])
    o_ref[...] = h + mod_ref[5:6, :] * y


def _shortconv_mixer(h, mods_l, g, w_in, conv_w, w_out):
    wb, wc, wh = (w_in[:, k * D_MODEL:(k + 1) * D_MODEL].astype(BF16) for k in range(3))
    row = lambda i: (i, 0)
    const = lambda i: (0, 0)
    wspec = _resident(wb.shape)
    return pl.pallas_call(
        _shortconv_kernel,
        out_shape=jax.ShapeDtypeStruct((N_TOK, D_MODEL), F32),
        grid=(N_TOK // ROW_TILE,),
        in_specs=[
            pl.BlockSpec((ROW_TILE, D_MODEL), row),
            pl.BlockSpec((None, N_MOD, D_MODEL), lambda i: (_mod_row(i, ROW_TILE), 0, 0)),
            pl.BlockSpec((1, D_MODEL), const),
            wspec, wspec, wspec,
            pl.BlockSpec(conv_w.shape, const),
            wspec,
        ],
        out_specs=pl.BlockSpec((ROW_TILE, D_MODEL), row),
        scratch_shapes=[pltpu.VMEM((ROW_TILE + 2 * HALO, D_MODEL), F32)],
        compiler_params=_params(("parallel",)),
        name="shortconv_mixer",
    )(h, mods_l, g.reshape(1, D_MODEL), wb, wc, wh, conv_w, w_out.astype(BF16))


def _gmlp_kernel(h_ref, mod_ref, g_ref, wu_ref, wv_ref, vn_ref, ws_ref, bs_ref, wo_ref, o_ref, a_ref):
    h = h_ref[...]
    u = _norm_mod(h, g_ref[...], mod_ref[3:4, :], mod_ref[4:5, :]).astype(BF16)
    zu = jax.nn.gelu(_dot(u, wu_ref[:, 0:GM_DIM]), approximate=True)
    zv = jax.nn.gelu(_dot(u, wv_ref[:, 0:GM_DIM]), approximate=True)
    zc = zv - jnp.mean(zv, axis=-1, keepdims=True)
    zv = zc * lax.rsqrt(jnp.mean(zc * zc, axis=-1, keepdims=True) + EPS) * vn_ref[...]
    zv = zv.astype(BF16)
    for c in range(ROW_TILE // GM_CHUNK):
        rs = slice(c * GM_CHUNK, (c + 1) * GM_CHUNK)
        for g in range(GM_GROUPS):
            gs = slice(g * GM_GROUP_DIM, (g + 1) * GM_GROUP_DIM)
            s = _dot(ws_ref[g], zv[rs, gs]) + bs_ref[:, g:g + 1]
            a_ref[rs, gs] = (zu[rs, gs] * s).astype(BF16)
    o_ref[...] = h + mod_ref[5:6, :] * _dot(a_ref[...], wo_ref[:, 0:D_MODEL])


def _gmlp_mixer(h, mods_l, g, w_in, v_norm, w_s, b_s, w_out):
    wu = _mxu_weight(w_in[:, :GM_DIM])
    wv = _mxu_weight(w_in[:, GM_DIM:])
    wo = _mxu_weight(w_out)
    row = lambda i: (i, 0)
    const = lambda i: (0, 0)
    return pl.pallas_call(
        _gmlp_kernel,
        out_shape=jax.ShapeDtypeStruct((N_TOK, D_MODEL), F32),
        grid=(N_TOK // ROW_TILE,),
        in_specs=[
            pl.BlockSpec((ROW_TILE, D_MODEL), row),
            pl.BlockSpec((None, N_MOD, D_MODEL), lambda i: (_mod_row(i, ROW_TILE), 0, 0)),
            pl.BlockSpec((1, D_MODEL), const),
            _resident(wu.shape),
            _resident(wv.shape),
            pl.BlockSpec((1, GM_DIM), const),
            pl.BlockSpec(w_s.shape, lambda i: (0, 0, 0)),
            pl.BlockSpec((GM_CHUNK, GM_GROUPS), const),
            _resident(wo.shape),
        ],
        out_specs=pl.BlockSpec((ROW_TILE, D_MODEL), row),
        scratch_shapes=[pltpu.VMEM((ROW_TILE, GM_DIM), BF16)],
        compiler_params=_params(("parallel",)),
        name="gmlp_mixer",
    )(h, mods_l, g.reshape(1, D_MODEL), wu, wv, v_norm.reshape(1, GM_DIM), w_s.astype(BF16),
      b_s.T, wo)


def kernel(x, c, ctx, c_ctx, ada_w, ada_b, norm_g, ffn_wg, ffn_wu, ffn_wd, ssd_in, ssd_conv_w, ssd_conv_b, ssd_dt_bias, ssd_a_log, ssd_d, ssd_norm, ssd_out, sc_in, sc_conv, sc_out, gm_in, gm_vnorm, gm_ws, gm_bs, gm_out, final_norm):
    h = (x.reshape(N_LAT, D_MODEL), ctx.reshape(N_CTX, D_MODEL))
    mods = _modulations(c, c_ctx, ada_w, ada_b)
    ffn_w = (ffn_wg.astype(BF16), ffn_wu.astype(BF16), ffn_wd.astype(BF16))
    for i in range(DEPTH):
        last = i == DEPTH - 1
        m = mods[i]
        h = _half_ffn(h, m, norm_g[i, 0], ffn_w, i, 0, N_TOK)
        kind, j = i % 3, i // 3
        if kind == 0:
            h = _ssd_mixer(h, m, norm_g[i, 1], ssd_in[j], ssd_conv_w[j], ssd_conv_b[j], ssd_dt_bias[j],
                           ssd_a_log[j], ssd_d[j], ssd_norm[j], ssd_out[j])
        elif kind == 1:
            h = _shortconv_mixer(h, m, norm_g[i, 1], sc_in[j], sc_conv[j], sc_out[j])
        else:
            h = _gmlp_mixer(h, m, norm_g[i, 1], gm_in[j], gm_vnorm[j], gm_ws[j], gm_bs[j], gm_out[j])
        h = _half_ffn(h, m, norm_g[i, 2], ffn_w, i, 1, N_LAT if last else N_TOK,
                      final_g=final_norm if last else None)
    return h.reshape(BATCH, SEQ, D_MODEL)
```

```python
import functools

import jax
import jax.numpy as jnp
import numpy as np
from jax import lax
from jax.experimental import pallas as pl
from jax.experimental.pallas import tpu as pltpu

F32 = jnp.float32
BF16 = jnp.bfloat16

D_MODEL = 1024
BATCH = 2
SEQ = 8192
DEPTH = 4
GRID_W = 64
CTX_LEN = 256
N_MOD = 9
EPS = 1e-6
LOG2E = 1.4426950408889634
FFN_DIM = 2816

SSD_INNER = 2048
SSD_HEADDIM = 64
SSD_HEADS = 32
SSD_STATE = 128
SSD_GROUPS = 4
SSD_HPG = 8
SSD_CONV = 5
SSD_CHUNK = 128
SSD_BC = SSD_GROUPS * SSD_STATE
SSD_CONV_DIM = SSD_INNER + 2 * SSD_BC
SSD_GROUP_W = SSD_HPG * SSD_HEADDIM

GM_CHUNK = 128
GM_DIM = 2048
GM_GROUPS = 8
GM_GROUP_DIM = 256

N_LAT = BATCH * SEQ
N_CTX = BATCH * CTX_LEN
N_TOK = N_LAT + N_CTX
CTX_MOD_ROW = BATCH

LANE = 128
SUBLANE = 8
VMEM_LIMIT = 56 * 1024 * 1024

FFN_TM = 512
FFN_FC = 256
ROW_TILE = 256
OUT_TM = 512
PREP_TILE = 512
HALO = SUBLANE
CONV_COLS = 256
SCAN_CHUNKS = 2
CONV_GROUPS = 8


def _dot(a, b):
    return jnp.dot(a, b, preferred_element_type=F32)


def _dot_nt(a, b):
    return lax.dot_general(a, b, (((1,), (1,)), ((), ())), preferred_element_type=F32)


def _mxu_weight(w):
    tiles = -(-w.shape[1] // LANE)
    pad = (tiles + (tiles + 1) % 2) * LANE - w.shape[1]
    return jnp.pad(w.astype(BF16), ((0, 0), (0, pad)))


def _silu(x):
    hx = 0.5 * x
    return hx + hx * jnp.tanh(hx)


def _mod_row(tile, tile_rows):
    return jnp.minimum((tile * tile_rows) // SEQ, CTX_MOD_ROW)


def _norm_mod(h, g, shift, scale):
    y = h * lax.rsqrt(jnp.mean(h * h, axis=-1, keepdims=True) + EPS) * g
    return y * (1.0 + scale) + shift


def _params(sem):
    return pltpu.CompilerParams(dimension_semantics=sem, vmem_limit_bytes=VMEM_LIMIT)


def _mod_kernel(ct_ref, w_ref, b_ref, o_ref, s_ref):
    n_vec = BATCH + 1
    tn = w_ref.shape[-1]

    @pl.when(jnp.logical_and(pl.program_id(0) == 0, pl.program_id(1) == 0))
    def _():
        s = _silu(ct_ref[...])
        for r in range(n_vec):
            s_ref[r] = jnp.broadcast_to(s[:, r:r + 1], (D_MODEL, LANE))

    def body(k, accs):
        r0 = pl.multiple_of(k * SUBLANE, SUBLANE)
        w8 = w_ref[pl.ds(r0, SUBLANE), :]
        out = []
        for r, a in enumerate(accs):
            s8 = s_ref[r, pl.ds(r0, SUBLANE), :]
            out.append(a + w8 * jnp.concatenate([s8] * (tn // LANE), axis=1))
        return tuple(out)

    zero = jnp.zeros((SUBLANE, tn), F32)
    accs = lax.fori_loop(0, D_MODEL // SUBLANE, body, (zero,) * n_vec, unroll=4)
    o_ref[...] = jnp.zeros(o_ref.shape, F32)
    for r, a in enumerate(accs):
        o_ref[r:r + 1, :] = jnp.sum(a, axis=0, keepdims=True) + b_ref[...]


def _modulations(c, c_ctx, ada_w, ada_b):
    ct = jnp.concatenate([c, c_ctx[None, :], jnp.zeros((SUBLANE - BATCH - 1, D_MODEL), F32)], axis=0).T
    tn = 1024
    n_out = N_MOD * D_MODEL
    out = pl.pallas_call(
        _mod_kernel,
        out_shape=jax.ShapeDtypeStruct((DEPTH, SUBLANE, n_out), F32),
        grid=(DEPTH, n_out // tn),
        in_specs=[
            pl.BlockSpec((D_MODEL, SUBLANE), lambda l, j: (0, 0)),
            pl.BlockSpec((None, D_MODEL, tn), lambda l, j: (l, 0, j)),
            pl.BlockSpec((None, 1, tn), lambda l, j: (l, 0, j)),
        ],
        out_specs=pl.BlockSpec((None, SUBLANE, tn), lambda l, j: (l, 0, j)),
        scratch_shapes=[pltpu.VMEM((BATCH + 1, D_MODEL, LANE), F32)],
        compiler_params=_params(("arbitrary", "arbitrary")),
        name="adaln_mod",
    )(ct, ada_w, ada_b.reshape(DEPTH, 1, n_out))
    return out.reshape(DEPTH, SUBLANE, N_MOD, D_MODEL)


def _ffn_kernel(*refs, mod_base, final, split_input):
    if split_input:
        x_ref, ctx_ref, *refs = refs
        is_ctx = pl.program_id(0) >= N_LAT // FFN_TM
        h = jnp.where(is_ctx, ctx_ref[...], x_ref[...])
    else:
        h_ref, *refs = refs
        h = h_ref[...]
    mod_ref, g_ref, wg_ref, wu_ref, wd_ref, *rest = refs
    if final:
        fg_ref, o_ref, a_ref = rest
    else:
        o_ref, a_ref = rest
    u = _norm_mod(h, g_ref[...], mod_ref[mod_base:mod_base + 1, :],
                  mod_ref[mod_base + 1:mod_base + 2, :]).astype(BF16)
    for c0 in range(0, FFN_DIM, FFN_FC):
        cs = slice(c0, c0 + FFN_FC)
        a_ref[:, cs] = (_silu(_dot(u, wg_ref[:, cs])) * _dot(u, wu_ref[:, cs])).astype(BF16)
    out = h + 0.5 * mod_ref[mod_base + 2:mod_base + 3, :] * _dot(a_ref[...], wd_ref[:, 0:D_MODEL])
    if final:
        out = out * lax.rsqrt(jnp.mean(out * out, axis=-1, keepdims=True) + EPS) * fg_ref[...]
    o_ref[...] = out


def _resident(shape):
    return pl.BlockSpec(shape, lambda *_: (0,) * len(shape), pipeline_mode=pl.Buffered(1))


def _half_ffn(h, mods_l, g, weights, layer, half, n_rows, final_g=None):
    final = final_g is not None
    split_input = isinstance(h, tuple)
    mod_base = 6 * half
    row = lambda i: (i, 0)
    if split_input:
        lat_tiles = N_LAT // FFN_TM
        h_specs = [pl.BlockSpec((FFN_TM, D_MODEL), lambda i: (jnp.minimum(i, lat_tiles - 1), 0)),
                   pl.BlockSpec((FFN_TM, D_MODEL), lambda i: (jnp.maximum(i - lat_tiles, 0), 0))]
        h_args = list(h)
    else:
        h_specs, h_args = [pl.BlockSpec((FFN_TM, D_MODEL), row)], [h]
    w_spec = lambda w: pl.BlockSpec((None, None) + w.shape[2:], lambda i: (layer, half, 0, 0),
                                    pipeline_mode=pl.Buffered(1))
    in_specs = h_specs + [
        pl.BlockSpec((None, N_MOD, D_MODEL), lambda i: (_mod_row(i, FFN_TM), 0, 0)),
        _resident((1, D_MODEL)),
    ] + [w_spec(w) for w in weights]
    args = h_args + [mods_l, g.reshape(1, D_MODEL)] + list(weights)
    if final:
        in_specs.append(_resident((1, D_MODEL)))
        args.append(final_g.reshape(1, D_MODEL))
    return pl.pallas_call(
        functools.partial(_ffn_kernel, mod_base=mod_base, final=final, split_input=split_input),
        out_shape=jax.ShapeDtypeStruct((n_rows, D_MODEL), F32),
        grid=(n_rows // FFN_TM,),
        in_specs=in_specs,
        out_specs=pl.BlockSpec((FFN_TM, D_MODEL), row),
        scratch_shapes=[pltpu.VMEM((FFN_TM, FFN_DIM), BF16)],
        compiler_params=_params(("parallel",)),
        name="half_ffn_final" if final else "half_ffn",
    )(*args)


def _ssd_out_kernel(h_ref, mod_ref, y_ref, z_ref, nw_ref, w_ref, o_ref, a_ref):
    for g in range(SSD_GROUPS):
        gs = slice(g * SSD_GROUP_W, (g + 1) * SSD_GROUP_W)
        y = y_ref[:, gs] * _silu(z_ref[:, gs])
        y = y * lax.rsqrt(jnp.mean(y * y, axis=-1, keepdims=True) + EPS) * nw_ref[:, gs]
        a_ref[:, gs] = y.astype(BF16)
    o_ref[...] = h_ref[...] + mod_ref[5:6, :] * _dot(a_ref[...], w_ref[:, 0:D_MODEL])


def _ssd_out(h, mods_l, y, z, norm_w, w):
    tm = OUT_TM
    row = lambda i: (i, 0)
    return pl.pallas_call(
        _ssd_out_kernel,
        out_shape=jax.ShapeDtypeStruct((N_TOK, D_MODEL), F32),
        grid=(N_TOK // tm,),
        in_specs=[
            pl.BlockSpec((tm, D_MODEL), row),
            pl.BlockSpec((None, N_MOD, D_MODEL), lambda i: (_mod_row(i, tm), 0, 0)),
            pl.BlockSpec((tm, SSD_INNER), row),
            pl.BlockSpec((tm, SSD_INNER), row),
            _resident((1, SSD_INNER)),
            _resident(w.shape),
        ],
        out_specs=pl.BlockSpec((tm, D_MODEL), row),
        scratch_shapes=[pltpu.VMEM((tm, SSD_INNER), BF16)],
        compiler_params=_params(("parallel",)),
        name="ssd_out_proj",
    )(h, mods_l, y, z, norm_w.reshape(1, SSD_INNER), w)


def _seq_edges(tile):
    lat_tiles = SEQ // ROW_TILE
    is_ctx = tile >= BATCH * lat_tiles
    pos = tile % lat_tiles
    return jnp.logical_or(is_ctx, pos == 0), jnp.logical_or(is_ctx, pos == lat_tiles - 1)


def _ssd_in_kernel(hp_ref, h_ref, hn_ref, mod_ref, g_ref, wz_ref, wx_ref, wdt_ref, cw_ref, cb_ref,
                   z_ref, x_ref, dt_ref):
    first, last = _seq_edges(pl.program_id(0))
    g, shift, scale = g_ref[...], mod_ref[3:4, :], mod_ref[4:5, :]
    u = _norm_mod(h_ref[...], g, shift, scale)
    u_ext = jnp.concatenate([_norm_mod(hp_ref[...], g, shift, scale), u,
                             _norm_mod(hn_ref[...], g, shift, scale)], axis=0).astype(BF16)
    ub = u.astype(BF16)
    n_grp = (ROW_TILE + 2 * HALO) // SUBLANE
    half = SSD_CONV // 2
    sub = lax.broadcasted_iota(jnp.int32, (1, SUBLANE, 1), 1)

    def project(blk):
        cols = slice(blk * CONV_COLS, (blk + 1) * CONV_COLS)
        e = _dot(u_ext, wx_ref[:, cols])
        e = jnp.concatenate([jnp.where(first, 0.0, e[0:HALO, :]), e[HALO:HALO + ROW_TILE, :],
                             jnp.where(last, 0.0, e[HALO + ROW_TILE:, :])], axis=0)
        return e.reshape(n_grp, SUBLANE, CONV_COLS)

    def conv(blk, e3):
        cols = slice(blk * CONV_COLS, (blk + 1) * CONV_COLS)
        for g0 in range(0, ROW_TILE // SUBLANE, CONV_GROUPS):
            ein = e3[g0:g0 + CONV_GROUPS + 2]
            acc = cb_ref[:, cols] + cw_ref[half:half + 1, cols] * ein[1:CONV_GROUPS + 1]
            for k in range(SSD_CONV):
                s = k - half
                if s == 0:
                    continue
                r3 = pltpu.roll(ein, (-s) % SUBLANE, axis=1)
                if s > 0:
                    shifted = jnp.where(sub < SUBLANE - s, r3[1:CONV_GROUPS + 1], r3[2:CONV_GROUPS + 2])
                else:
                    shifted = jnp.where(sub >= -s, r3[1:CONV_GROUPS + 1], r3[0:CONV_GROUPS])
                acc = acc + cw_ref[k:k + 1, cols] * shifted
            rows = slice(g0 * SUBLANE, (g0 + CONV_GROUPS) * SUBLANE)
            x_ref[rows, cols] = _silu(acc).reshape(CONV_GROUPS * SUBLANE, CONV_COLS)

    n_x = SSD_CONV_DIM // CONV_COLS
    n_z = SSD_INNER // CONV_COLS
    e3 = project(0)
    for blk in range(n_x):
        e3_next = project(blk + 1) if blk + 1 < n_x else None
        if blk < n_z:
            cols = slice(blk * CONV_COLS, (blk + 1) * CONV_COLS)
            z_ref[:, cols] = _dot(ub, wz_ref[:, cols])
        elif blk == n_z:
            dt_ref[...] = _dot(ub, wdt_ref[...])
        conv(blk, e3)
        e3 = e3_next


def _ssd_in(h, mods_l, g, wz, wx, wdt, conv_w, conv_b):
    per = ROW_TILE // HALO
    n_halo = N_TOK // HALO
    row = lambda i: (i, 0)
    widths = (SSD_INNER, SSD_CONV_DIM, LANE)
    return pl.pallas_call(
        _ssd_in_kernel,
        out_shape=[jax.ShapeDtypeStruct((N_TOK, n), F32) for n in widths],
        grid=(N_TOK // ROW_TILE,),
        in_specs=[
            pl.BlockSpec((HALO, D_MODEL), lambda i: (jnp.maximum(i * per - 1, 0), 0)),
            pl.BlockSpec((ROW_TILE, D_MODEL), row),
            pl.BlockSpec((HALO, D_MODEL), lambda i: (jnp.minimum((i + 1) * per, n_halo - 1), 0)),
            pl.BlockSpec((None, N_MOD, D_MODEL), lambda i: (_mod_row(i, ROW_TILE), 0, 0)),
            _resident((1, D_MODEL)),
            _resident(wz.shape), _resident(wx.shape), _resident(wdt.shape),
            _resident(conv_w.shape), _resident((1, SSD_CONV_DIM)),
        ],
        out_specs=[pl.BlockSpec((ROW_TILE, n), row) for n in widths],
        compiler_params=_params(("parallel",)),
        name="ssd_in_proj",
    )(h, h, h, mods_l, g.reshape(1, D_MODEL), wz, wx, wdt, conv_w, conv_b.reshape(1, SSD_CONV_DIM))


def _split_hi_lo(x):
    hi = x.astype(BF16)
    lo = (x - hi.astype(F32)).astype(BF16)
    return jnp.concatenate([hi, lo], axis=1)


def _ssd_prep_kernel(dt_ref, bias_ref, alog_ref, cs2_ref, rowt_ref, ws_ref):
    q = SSD_CHUNK
    ii = lax.broadcasted_iota(jnp.int32, (q, q), 0)
    jj = lax.broadcasted_iota(jnp.int32, (q, q), 1)
    lower = (jj <= ii).astype(F32)
    upper = (jj >= ii).astype(F32)
    is_fwd = lax.broadcasted_iota(jnp.int32, (q, LANE), 1) < SSD_HEADS
    neg_a = -jnp.exp(alog_ref[...])
    for c in range(PREP_TILE // q):
        rows = slice(c * q, (c + 1) * q)
        x_raw = dt_ref[rows, :] + bias_ref[...]
        dt = jnp.maximum(x_raw, 0.0) + jnp.log1p(jnp.exp(-jnp.abs(x_raw)))
        da = dt * neg_a
        cs_f = jnp.dot(lower, da, precision=lax.Precision.HIGHEST, preferred_element_type=F32)
        cs_b = jnp.dot(upper, da, precision=lax.Precision.HIGHEST, preferred_element_type=F32)
        cs = jnp.where(is_fwd, cs_f, cs_b)
        total = jnp.where(is_fwd[0:1, :], cs_f[q - 1:q, :], cs_b[0:1, :])
        cs2 = cs * LOG2E
        cs2_ref[rows, :] = cs2
        rowt_ref[rows, :] = (cs2 - jnp.log2(dt)).T
        w_end = dt * jnp.exp(total - cs)
        chunk_decay = jnp.broadcast_to(jnp.exp(total), (2 * SUBLANE, LANE))
        ws_ref[c] = _split_hi_lo(jnp.concatenate([w_end, chunk_decay], axis=0))


def _ssd_prep(dt_raw, bias_pad, alog_pad):
    q = SSD_CHUNK
    per = PREP_TILE // q
    row = lambda i: (i, 0)
    const = lambda i: (0, 0)
    return pl.pallas_call(
        _ssd_prep_kernel,
        out_shape=[jax.ShapeDtypeStruct((N_TOK, LANE), F32), jax.ShapeDtypeStruct((N_TOK, LANE), F32),
                   jax.ShapeDtypeStruct((N_TOK // q, q + 2 * SUBLANE, 2 * LANE), BF16)],
        grid=(N_TOK // PREP_TILE,),
        in_specs=[pl.BlockSpec((PREP_TILE, LANE), row), pl.BlockSpec((1, LANE), const),
                  pl.BlockSpec((1, LANE), const)],
        out_specs=[pl.BlockSpec((PREP_TILE, LANE), row), pl.BlockSpec((PREP_TILE, LANE), row),
                   pl.BlockSpec((per, q + 2 * SUBLANE, 2 * LANE), lambda i: (i, 0, 0))],
        compiler_params=_params(("parallel",)),
        name="ssd_prep",
    )(dt_raw, bias_pad, alog_pad)


def _ssd_scan_kernel(xbc_ref, cs2_ref, rowt_ref, ws_ref, e2_ref, *rest, reverse):
    if reverse:
        yf_ref, dexp_ref, o_ref, st_ref, lhs_ref = rest
    else:
        o_ref, st_ref, lhs_ref = rest
    q = SSD_CHUNK
    off = SSD_HEADS if reverse else 0

    @pl.when(pl.program_id(1) == 0)
    def _():
        st_ref[...] = jnp.zeros(st_ref.shape, F32)

    ii = lax.broadcasted_iota(jnp.int32, (q, q), 0)
    jj = lax.broadcasted_iota(jnp.int32, (q, q), 1)
    tri = (jj >= ii) if reverse else (jj <= ii)
    lane = lax.broadcasted_iota(jnp.int32, (q, LANE), 1)

    pairs = SSD_GROUP_W // LANE

    def build(c, g):
        r0 = c * q
        rows = slice(r0, r0 + q)
        b_g = xbc_ref[rows, SSD_INNER + g * SSD_STATE:SSD_INNER + (g + 1) * SSD_STATE]
        c_lo = SSD_INNER + SSD_BC + g * SSD_STATE
        c_g = xbc_ref[rows, c_lo:c_lo + SSD_STATE]
        cb = _dot_nt(c_g.astype(BF16), b_g.astype(BF16))
        for k in range(pairs):
            buf = lhs_ref.at[c, g * pairs + k]
            for hh, h in enumerate((g * SSD_HPG + 2 * k, g * SSD_HPG + 2 * k + 1)):
                col = off + h
                cs_i = jnp.broadcast_to(cs2_ref[rows, col:col + 1], (q, q))
                row_j = rowt_ref[r0 + col:r0 + col + 1, :]
                m = jnp.exp2(jnp.where(tri, cs_i - row_j, -jnp.inf)) * cb
                c_scaled = c_g * jnp.exp2(cs_i)
                buf[hh * q:(hh + 1) * q, 0:q] = m.astype(BF16)
                buf[hh * q:(hh + 1) * q, q:2 * q] = c_scaled.astype(BF16)

    def apply(c, g, spread):
        rows = slice(c * q, (c + 1) * q)
        b_g = xbc_ref[rows, SSD_INNER + g * SSD_STATE:SSD_INNER + (g + 1) * SSD_STATE]
        gs = slice(g * SSD_GROUP_W, (g + 1) * SSD_GROUP_W)
        st_old = st_ref[:, gs]
        st_bf = st_old.astype(BF16)
        x_g = xbc_ref[rows, gs]
        for k in range(pairs):
            ls = slice(k * LANE, (k + 1) * LANE)
            rhs = jnp.concatenate([x_g[:, ls].astype(BF16), st_bf[:, ls]], axis=0)
            out = _dot(lhs_ref[c, g * pairs + k], rhs)
            y_pair = jnp.where(lane < SSD_HEADDIM, out[0:q, :], out[q:2 * q, :])
            cols = slice(g * SSD_GROUP_W + k * LANE, g * SSD_GROUP_W + (k + 1) * LANE)
            if reverse:
                y_pair = y_pair + yf_ref[rows, cols] + dexp_ref[:, cols] * x_g[:, ls]
            o_ref[rows, cols] = y_pair
        xw = (x_g * spread[0:q, gs]).astype(BF16)
        st_ref[:, gs] = st_old * spread[q:q + 1, gs] + _dot(b_g.T.astype(BF16), xw)

    order = list(range(SCAN_CHUNKS - 1, -1, -1) if reverse else range(SCAN_CHUNKS))
    for g in range(SSD_GROUPS):
        build(order[0], g)
    for n, c in enumerate(order):
        spread = _dot(ws_ref[c], e2_ref[:, 0:SSD_INNER])
        for g in range(SSD_GROUPS):
            if n + 1 < len(order):
                build(order[n + 1], g)
            apply(c, g, spread)


def _ssd_scan(xbc_act, cs2, rowt, ws, e2, reverse, extra=()):
    block = SCAN_CHUNKS * SSD_CHUNK
    lat_steps = SEQ // block
    ctx_base = N_LAT // block
    steps = 1 + lat_steps

    def blk(b, s):
        lat = (lat_steps - s) if reverse else (s - 1)
        return jnp.where(s == 0, ctx_base + b, b * lat_steps + lat)

    rows = lambda w: pl.BlockSpec((block, w), lambda b, s: (blk(b, s), 0))
    const = lambda shape: pl.BlockSpec(shape, lambda b, s: (0, 0))
    ws_spec = pl.BlockSpec((SCAN_CHUNKS,) + ws.shape[1:], lambda b, s: (blk(b, s), 0, 0))
    in_specs = [rows(SSD_CONV_DIM), rows(LANE), rows(LANE), ws_spec, const(e2.shape)]
    args = [xbc_act, cs2, rowt, ws, e2]
    if reverse:
        y_f, d_exp = extra
        in_specs += [rows(SSD_INNER), const((1, SSD_INNER))]
        args += [y_f, d_exp]
    scratch = [pltpu.VMEM((SSD_STATE, SSD_INNER), F32),
               pltpu.VMEM((SCAN_CHUNKS, SSD_HEADS // 2, 2 * SSD_CHUNK, 2 * SSD_CHUNK), BF16)]
    return pl.pallas_call(
        functools.partial(_ssd_scan_kernel, reverse=reverse),
        out_shape=jax.ShapeDtypeStruct((N_TOK, SSD_INNER), F32),
        grid=(BATCH, steps),
        in_specs=in_specs,
        out_specs=rows(SSD_INNER),
        scratch_shapes=scratch,
        compiler_params=_params(("parallel", "arbitrary")),
        name="ssd_scan_bwd" if reverse else "ssd_scan_fwd",
    )(*args)


def _head_spread_matrix(off):
    e = np.zeros((LANE, SSD_INNER), np.float32)
    for h in range(SSD_HEADS):
        e[off + h, h * SSD_HEADDIM:(h + 1) * SSD_HEADDIM] = 1.0
    return _mxu_weight(jnp.asarray(np.concatenate([e, e], axis=0)))


def _ssd_mixer(h, mods_l, g, w_in, conv_w, conv_b, dt_bias, a_log, d_skip, norm_w, w_out):
    wz = _mxu_weight(w_in[:, :SSD_INNER])
    wxbc = _mxu_weight(w_in[:, SSD_INNER:SSD_INNER + SSD_CONV_DIM])
    n_dt = 2 * SSD_HEADS
    wdt = _mxu_weight(w_in[:, SSD_INNER + SSD_CONV_DIM:])
    z, xbc_act, dt_raw = _ssd_in(h, mods_l, g, wz, wxbc, wdt, conv_w, conv_b)
    bias_pad = jnp.pad(dt_bias.reshape(1, n_dt), ((0, 0), (0, LANE - n_dt)))
    alog_pad = jnp.pad(a_log.reshape(1, n_dt), ((0, 0), (0, LANE - n_dt)))
    cs2, rowt, ws = _ssd_prep(dt_raw, bias_pad, alog_pad)
    y_f = _ssd_scan(xbc_act, cs2, rowt, ws, _head_spread_matrix(0), False)
    d_exp = jnp.repeat(d_skip, SSD_HEADDIM).reshape(1, SSD_INNER)
    y = _ssd_scan(xbc_act, cs2, rowt, ws, _head_spread_matrix(SSD_HEADS), True,
                  extra=(y_f, d_exp))
    return _ssd_out(h, mods_l, y, z, norm_w, w_out.astype(BF16))


def _shortconv_kernel(h_ref, mod_ref, g_ref, wb_ref, wc_ref, wh_ref, cw_ref, wo_ref, o_ref, v_ref):
    h = h_ref[...]
    u = _norm_mod(h, g_ref[...], mod_ref[3:4, :], mod_ref[4:5, :]).astype(BF16)
    v = _dot(u, wc_ref[:, 0:D_MODEL]) * _dot(u, wh_ref[:, 0:D_MODEL])
    v_ref[0:HALO, :] = jnp.zeros((HALO, D_MODEL), F32)
    v_ref[HALO:HALO + ROW_TILE, :] = v
    v_ref[HALO + ROW_TILE:, :] = jnp.zeros((HALO, D_MODEL), F32)
    is_ctx = pl.program_id(0) >= N_LAT // ROW_TILE
    row_len = jnp.where(is_ctx, CTX_LEN, GRID_W)
    pos = lax.broadcasted_iota(jnp.int32, (ROW_TILE, 1), 0) & (row_len - 1)
    v_prev = jnp.where(pos == 0, 0.0, v_ref[HALO - 1:HALO - 1 + ROW_TILE, :])
    v_next = jnp.where(pos == row_len - 1, 0.0, v_ref[HALO + 1:HALO + 1 + ROW_TILE, :])
    conv = cw_ref[0:1, :] * v_prev + cw_ref[1:2, :] * v + cw_ref[2:3, :] * v_next
    y = _dot((_dot(u, wb_ref[:, 0:D_MODEL]) * conv).astype(BF16), wo_ref[:, 0:D_MODEL])
    o_ref[...] = h + mod_ref[5:6, :] * y


def _shortconv_mixer(h, mods_l, g, w_in, conv_w, w_out):
    wb, wc, wh = (w_in[:, k * D_MODEL:(k + 1) * D_MODEL].astype(BF16) for k in range(3))
    row = lambda i: (i, 0)
    const = lambda i: (0, 0)
    wspec = _resident(wb.shape)
    return pl.pallas_call(
        _shortconv_kernel,
        out_shape=jax.ShapeDtypeStruct((N_TOK, D_MODEL), F32),
        grid=(N_TOK // ROW_TILE,),
        in_specs=[
            pl.BlockSpec((ROW_TILE, D_MODEL), row),
            pl.BlockSpec((None, N_MOD, D_MODEL), lambda i: (_mod_row(i, ROW_TILE), 0, 0)),
            pl.BlockSpec((1, D_MODEL), const),
            wspec, wspec, wspec,
            pl.BlockSpec(conv_w.shape, const),
            wspec,
        ],
        out_specs=pl.BlockSpec((ROW_TILE, D_MODEL), row),
        scratch_shapes=[pltpu.VMEM((ROW_TILE + 2 * HALO, D_MODEL), F32)],
        compiler_params=_params(("parallel",)),
        name="shortconv_mixer",
    )(h, mods_l, g.reshape(1, D_MODEL), wb, wc, wh, conv_w, w_out.astype(BF16))


def _gmlp_kernel(h_ref, mod_ref, g_ref, wu_ref, wv_ref, vn_ref, ws_ref, bs_ref, wo_ref, o_ref, a_ref):
    h = h_ref[...]
    u = _norm_mod(h, g_ref[...], mod_ref[3:4, :], mod_ref[4:5, :]).astype(BF16)
    zu = jax.nn.gelu(_dot(u, wu_ref[:, 0:GM_DIM]), approximate=True)
    zv = jax.nn.gelu(_dot(u, wv_ref[:, 0:GM_DIM]), approximate=True)
    zc = zv - jnp.mean(zv, axis=-1, keepdims=True)
    zv = zc * lax.rsqrt(jnp.mean(zc * zc, axis=-1, keepdims=True) + EPS) * vn_ref[...]
    zv = zv.astype(BF16)
    for c in range(ROW_TILE // GM_CHUNK):
        rs = slice(c * GM_CHUNK, (c + 1) * GM_CHUNK)
        for g in range(GM_GROUPS):
            gs = slice(g * GM_GROUP_DIM, (g + 1) * GM_GROUP_DIM)
            s = _dot(ws_ref[g], zv[rs, gs]) + bs_ref[:, g:g + 1]
            a_ref[rs, gs] = (zu[rs, gs] * s).astype(BF16)
    o_ref[...] = h + mod_ref[5:6, :] * _dot(a_ref[...], wo_ref[:, 0:D_MODEL])


def _gmlp_mixer(h, mods_l, g, w_in, v_norm, w_s, b_s, w_out):
    wu = _mxu_weight(w_in[:, :GM_DIM])
    wv = _mxu_weight(w_in[:, GM_DIM:])
    wo = _mxu_weight(w_out)
    row = lambda i: (i, 0)
    const = lambda i: (0, 0)
    return pl.pallas_call(
        _gmlp_kernel,
        out_shape=jax.ShapeDtypeStruct((N_TOK, D_MODEL), F32),
        grid=(N_TOK // ROW_TILE,),
        in_specs=[
            pl.BlockSpec((ROW_TILE, D_MODEL), row),
            pl.BlockSpec((None, N_MOD, D_MODEL), lambda i: (_mod_row(i, ROW_TILE), 0, 0)),
            pl.BlockSpec((1, D_MODEL), const),
            _resident(wu.shape),
            _resident(wv.shape),
            pl.BlockSpec((1, GM_DIM), const),
            pl.BlockSpec(w_s.shape, lambda i: (0, 0, 0)),
            pl.BlockSpec((GM_CHUNK, GM_GROUPS), const),
            _resident(wo.shape),
        ],
        out_specs=pl.BlockSpec((ROW_TILE, D_MODEL), row),
        scratch_shapes=[pltpu.VMEM((ROW_TILE, GM_DIM), BF16)],
        compiler_params=_params(("parallel",)),
        name="gmlp_mixer",
    )(h, mods_l, g.reshape(1, D_MODEL), wu, wv, v_norm.reshape(1, GM_DIM), w_s.astype(BF16),
      b_s.T, wo)


def kernel(x, c, ctx, c_ctx, ada_w, ada_b, norm_g, ffn_wg, ffn_wu, ffn_wd, ssd_in, ssd_conv_w, ssd_conv_b, ssd_dt_bias, ssd_a_log, ssd_d, ssd_norm, ssd_out, sc_in, sc_conv, sc_out, gm_in, gm_vnorm, gm_ws, gm_bs, gm_out, final_norm):
    h = (x.reshape(N_LAT, D_MODEL), ctx.reshape(N_CTX, D_MODEL))
    mods = _modulations(c, c_ctx, ada_w, ada_b)
    ffn_w = (ffn_wg.astype(BF16), ffn_wu.astype(BF16), ffn_wd.astype(BF16))
    for i in range(DEPTH):
        last = i == DEPTH - 1
        m = mods[i]
        h = _half_ffn(h, m, norm_g[i, 0], ffn_w, i, 0, N_TOK)
        kind, j = i % 3, i // 3
        if kind == 0:
            h = _ssd_mixer(h, m, norm_g[i, 1], ssd_in[j], ssd_conv_w[j], ssd_conv_b[j], ssd_dt_bias[j],
                           ssd_a_log[j], ssd_d[j], ssd_norm[j], ssd_out[j])
        elif kind == 1:
            h = _shortconv_mixer(h, m, norm_g[i, 1], sc_in[j], sc_conv[j], sc_out[j])
        else:
            h = _gmlp_mixer(h, m, norm_g[i, 1], gm_in[j], gm_vnorm[j], gm_ws[j], gm_bs[j], gm_out[j])
        h = _half_ffn(h, m, norm_g[i, 2], ffn_w, i, 1, N_LAT if last else N_TOK,
                      final_g=final_norm if last else None)
    return h.reshape(BATCH, SEQ, D_MODEL)
```

```python
import functools

import jax
import jax.numpy as jnp
import numpy as np
from jax import lax
from jax.experimental import pallas as pl
from jax.experimental.pallas import tpu as pltpu

F32 = jnp.float32
BF16 = jnp.bfloat16

D_MODEL = 1024
BATCH = 2
SEQ = 8192
DEPTH = 4
GRID_W = 64
CTX_LEN = 256
N_MOD = 9
EPS = 1e-6
LOG2E = 1.4426950408889634
FFN_DIM = 2816

SSD_INNER = 2048
SSD_HEADDIM = 64
SSD_HEADS = 32
SSD_STATE = 128
SSD_GROUPS = 4
SSD_HPG = 8
SSD_CONV = 5
SSD_CHUNK = 128
SSD_BC = SSD_GROUPS * SSD_STATE
SSD_CONV_DIM = SSD_INNER + 2 * SSD_BC
SSD_GROUP_W = SSD_HPG * SSD_HEADDIM

GM_CHUNK = 128
GM_DIM = 2048
GM_GROUPS = 8
GM_GROUP_DIM = 256

N_LAT = BATCH * SEQ
N_CTX = BATCH * CTX_LEN
N_TOK = N_LAT + N_CTX
CTX_MOD_ROW = BATCH

LANE = 128
SUBLANE = 8
VMEM_LIMIT = 56 * 1024 * 1024

FFN_TM = 512
FFN_FC = 256
ROW_TILE = 256
PREP_TILE = 512
HALO = SUBLANE
CONV_COLS = 256
SCAN_CHUNKS = 2
CONV_GROUPS = 8


def _dot(a, b):
    return jnp.dot(a, b, preferred_element_type=F32)


def _dot_nt(a, b):
    return lax.dot_general(a, b, (((1,), (1,)), ((), ())), preferred_element_type=F32)


def _mxu_weight(w):
    tiles = -(-w.shape[1] // LANE)
    pad = (tiles + (tiles + 1) % 2) * LANE - w.shape[1]
    return jnp.pad(w.astype(BF16), ((0, 0), (0, pad)))


def _silu(x):
    hx = 0.5 * x
    return hx + hx * jnp.tanh(hx)


def _mod_row(tile, tile_rows):
    return jnp.minimum((tile * tile_rows) // SEQ, CTX_MOD_ROW)


def _norm_mod(h, g, shift, scale):
    y = h * lax.rsqrt(jnp.mean(h * h, axis=-1, keepdims=True) + EPS) * g
    return y * (1.0 + scale) + shift


def _params(sem):
    return pltpu.CompilerParams(dimension_semantics=sem, vmem_limit_bytes=VMEM_LIMIT)


def _mod_kernel(ct_ref, w_ref, b_ref, o_ref, s_ref):
    n_vec = BATCH + 1
    tn = w_ref.shape[-1]

    @pl.when(jnp.logical_and(pl.program_id(0) == 0, pl.program_id(1) == 0))
    def _():
        s = _silu(ct_ref[...])
        for r in range(n_vec):
            s_ref[r] = jnp.broadcast_to(s[:, r:r + 1], (D_MODEL, LANE))

    def body(k, accs):
        r0 = pl.multiple_of(k * SUBLANE, SUBLANE)
        w8 = w_ref[pl.ds(r0, SUBLANE), :]
        out = []
        for r, a in enumerate(accs):
            s8 = s_ref[r, pl.ds(r0, SUBLANE), :]
            out.append(a + w8 * jnp.concatenate([s8] * (tn // LANE), axis=1))
        return tuple(out)

    zero = jnp.zeros((SUBLANE, tn), F32)
    accs = lax.fori_loop(0, D_MODEL // SUBLANE, body, (zero,) * n_vec, unroll=4)
    o_ref[...] = jnp.zeros(o_ref.shape, F32)
    for r, a in enumerate(accs):
        o_ref[r:r + 1, :] = jnp.sum(a, axis=0, keepdims=True) + b_ref[...]


def _modulations(c, c_ctx, ada_w, ada_b):
    ct = jnp.concatenate([c, c_ctx[None, :], jnp.zeros((SUBLANE - BATCH - 1, D_MODEL), F32)], axis=0).T
    tn = 1024
    n_out = N_MOD * D_MODEL
    out = pl.pallas_call(
        _mod_kernel,
        out_shape=jax.ShapeDtypeStruct((DEPTH, SUBLANE, n_out), F32),
        grid=(DEPTH, n_out // tn),
        in_specs=[
            pl.BlockSpec((D_MODEL, SUBLANE), lambda l, j: (0, 0)),
            pl.BlockSpec((None, D_MODEL, tn), lambda l, j: (l, 0, j)),
            pl.BlockSpec((None, 1, tn), lambda l, j: (l, 0, j)),
        ],
        out_specs=pl.BlockSpec((None, SUBLANE, tn), lambda l, j: (l, 0, j)),
        scratch_shapes=[pltpu.VMEM((BATCH + 1, D_MODEL, LANE), F32)],
        compiler_params=_params(("arbitrary", "arbitrary")),
        name="adaln_mod",
    )(ct, ada_w, ada_b.reshape(DEPTH, 1, n_out))
    return out.reshape(DEPTH, SUBLANE, N_MOD, D_MODEL)


def _ffn_kernel(*refs, mod_base, final, split_input, ssd_tail):
    if split_input:
        x_ref, ctx_ref, *refs = refs
        is_ctx = pl.program_id(0) >= N_LAT // FFN_TM
        h = jnp.where(is_ctx, ctx_ref[...], x_ref[...])
    else:
        h_ref, *refs = refs
        h = h_ref[...]
    if ssd_tail:
        y_ref, z_ref, nw_ref, wo_ref, *refs = refs
    mod_ref, g_ref, wg_ref, wu_ref, wd_ref, *rest = refs
    if final:
        fg_ref, o_ref, a_ref, *rest = rest
    else:
        o_ref, a_ref, *rest = rest
    if ssd_tail:
        ya_ref, = rest
        for g in range(SSD_GROUPS):
            gs = slice(g * SSD_GROUP_W, (g + 1) * SSD_GROUP_W)
            y = y_ref[:, gs].astype(F32) * _silu(z_ref[:, gs].astype(F32))
            y = y * lax.rsqrt(jnp.mean(y * y, axis=-1, keepdims=True) + EPS) * nw_ref[:, gs]
            ya_ref[:, gs] = y.astype(BF16)
        h = h + mod_ref[5:6, :] * _dot(ya_ref[...], wo_ref[:, 0:D_MODEL])
    u = _norm_mod(h, g_ref[...], mod_ref[mod_base:mod_base + 1, :],
                  mod_ref[mod_base + 1:mod_base + 2, :]).astype(BF16)
    for c0 in range(0, FFN_DIM, FFN_FC):
        cs = slice(c0, c0 + FFN_FC)
        a_ref[:, cs] = (_silu(_dot(u, wg_ref[:, cs])) * _dot(u, wu_ref[:, cs])).astype(BF16)
    out = h + 0.5 * mod_ref[mod_base + 2:mod_base + 3, :] * _dot(a_ref[...], wd_ref[:, 0:D_MODEL])
    if final:
        out = out * lax.rsqrt(jnp.mean(out * out, axis=-1, keepdims=True) + EPS) * fg_ref[...]
    o_ref[...] = out


def _resident(shape):
    return pl.BlockSpec(shape, lambda *_: (0,) * len(shape), pipeline_mode=pl.Buffered(1))


def _half_ffn(h, mods_l, g, weights, layer, half, n_rows, final_g=None, ssd_tail=None):
    final = final_g is not None
    split_input = isinstance(h, tuple)
    mod_base = 6 * half
    row = lambda i: (i, 0)
    if split_input:
        lat_tiles = N_LAT // FFN_TM
        h_specs = [pl.BlockSpec((FFN_TM, D_MODEL), lambda i: (jnp.minimum(i, lat_tiles - 1), 0)),
                   pl.BlockSpec((FFN_TM, D_MODEL), lambda i: (jnp.maximum(i - lat_tiles, 0), 0))]
        h_args = list(h)
    else:
        h_specs, h_args = [pl.BlockSpec((FFN_TM, D_MODEL), row)], [h]
    w_spec = lambda w: pl.BlockSpec((None, None) + w.shape[2:], lambda i: (layer, half, 0, 0),
                                    pipeline_mode=pl.Buffered(1))
    scratch = [pltpu.VMEM((FFN_TM, FFN_DIM), BF16)]
    if ssd_tail is not None:
        y, z, norm_w, w_out = ssd_tail
        h_specs += [pl.BlockSpec((FFN_TM, SSD_INNER), row), pl.BlockSpec((FFN_TM, SSD_INNER), row),
                    _resident((1, SSD_INNER)), _resident(w_out.shape)]
        h_args += [y, z, norm_w.reshape(1, SSD_INNER), w_out]
        scratch.append(pltpu.VMEM((FFN_TM, SSD_INNER), BF16))
    in_specs = h_specs + [
        pl.BlockSpec((None, N_MOD, D_MODEL), lambda i: (_mod_row(i, FFN_TM), 0, 0)),
        _resident((1, D_MODEL)),
    ] + [w_spec(w) for w in weights]
    args = h_args + [mods_l, g.reshape(1, D_MODEL)] + list(weights)
    if final:
        in_specs.append(_resident((1, D_MODEL)))
        args.append(final_g.reshape(1, D_MODEL))
    return pl.pallas_call(
        functools.partial(_ffn_kernel, mod_base=mod_base, final=final, split_input=split_input,
                          ssd_tail=ssd_tail is not None),
        out_shape=jax.ShapeDtypeStruct((n_rows, D_MODEL), F32),
        grid=(n_rows // FFN_TM,),
        in_specs=in_specs,
        out_specs=pl.BlockSpec((FFN_TM, D_MODEL), row),
        scratch_shapes=scratch,
        compiler_params=_params(("parallel",)),
        name=("ssd_tail_" if ssd_tail is not None else "") + ("half_ffn_final" if final else "half_ffn"),
    )(*args)


def _seq_edges(tile):
    lat_tiles = SEQ // ROW_TILE
    is_ctx = tile >= BATCH * lat_tiles
    pos = tile % lat_tiles
    return jnp.logical_or(is_ctx, pos == 0), jnp.logical_or(is_ctx, pos == lat_tiles - 1)


def _ssd_in_kernel(hp_ref, h_ref, hn_ref, mod_ref, g_ref, wz_ref, wx_ref, wdt_ref, cw_ref, cb_ref,
                   z_ref, x_ref, dt_ref):
    first, last = _seq_edges(pl.program_id(0))
    g, shift, scale = g_ref[...], mod_ref[3:4, :], mod_ref[4:5, :]
    u = _norm_mod(h_ref[...], g, shift, scale)
    u_ext = jnp.concatenate([_norm_mod(hp_ref[...], g, shift, scale), u,
                             _norm_mod(hn_ref[...], g, shift, scale)], axis=0).astype(BF16)
    ub = u.astype(BF16)
    n_grp = (ROW_TILE + 2 * HALO) // SUBLANE
    half = SSD_CONV // 2
    sub = lax.broadcasted_iota(jnp.int32, (1, SUBLANE, 1), 1)

    def project(blk):
        cols = slice(blk * CONV_COLS, (blk + 1) * CONV_COLS)
        e = _dot(u_ext, wx_ref[:, cols])
        e = jnp.concatenate([jnp.where(first, 0.0, e[0:HALO, :]), e[HALO:HALO + ROW_TILE, :],
                             jnp.where(last, 0.0, e[HALO + ROW_TILE:, :])], axis=0)
        return e.reshape(n_grp, SUBLANE, CONV_COLS)

    def conv(blk, e3):
        cols = slice(blk * CONV_COLS, (blk + 1) * CONV_COLS)
        for g0 in range(0, ROW_TILE // SUBLANE, CONV_GROUPS):
            ein = e3[g0:g0 + CONV_GROUPS + 2]
            acc = cb_ref[:, cols] + cw_ref[half:half + 1, cols] * ein[1:CONV_GROUPS + 1]
            for k in range(SSD_CONV):
                s = k - half
                if s == 0:
                    continue
                r3 = pltpu.roll(ein, (-s) % SUBLANE, axis=1)
                if s > 0:
                    shifted = jnp.where(sub < SUBLANE - s, r3[1:CONV_GROUPS + 1], r3[2:CONV_GROUPS + 2])
                else:
                    shifted = jnp.where(sub >= -s, r3[1:CONV_GROUPS + 1], r3[0:CONV_GROUPS])
                acc = acc + cw_ref[k:k + 1, cols] * shifted
            rows = slice(g0 * SUBLANE, (g0 + CONV_GROUPS) * SUBLANE)
            x_ref[rows, cols] = _silu(acc).reshape(CONV_GROUPS * SUBLANE, CONV_COLS).astype(x_ref.dtype)

    n_x = SSD_CONV_DIM // CONV_COLS
    n_z = SSD_INNER // CONV_COLS
    e3 = project(0)
    for blk in range(n_x):
        e3_next = project(blk + 1) if blk + 1 < n_x else None
        if blk < n_z:
            cols = slice(blk * CONV_COLS, (blk + 1) * CONV_COLS)
            z_ref[:, cols] = _dot(ub, wz_ref[:, cols]).astype(z_ref.dtype)
        elif blk == n_z:
            dt_ref[...] = _dot(ub, wdt_ref[...])
        conv(blk, e3)
        e3 = e3_next


def _ssd_in(h, mods_l, g, wz, wx, wdt, conv_w, conv_b):
    per = ROW_TILE // HALO
    n_halo = N_TOK // HALO
    row = lambda i: (i, 0)
    widths = (SSD_INNER, SSD_CONV_DIM, LANE)
    return pl.pallas_call(
        _ssd_in_kernel,
        out_shape=[jax.ShapeDtypeStruct((N_TOK, n), dt) for n, dt in zip(widths, (BF16, F32, F32))],
        grid=(N_TOK // ROW_TILE,),
        in_specs=[
            pl.BlockSpec((HALO, D_MODEL), lambda i: (jnp.maximum(i * per - 1, 0), 0)),
            pl.BlockSpec((ROW_TILE, D_MODEL), row),
            pl.BlockSpec((HALO, D_MODEL), lambda i: (jnp.minimum((i + 1) * per, n_halo - 1), 0)),
            pl.BlockSpec((None, N_MOD, D_MODEL), lambda i: (_mod_row(i, ROW_TILE), 0, 0)),
            _resident((1, D_MODEL)),
            _resident(wz.shape), _resident(wx.shape), _resident(wdt.shape),
            _resident(conv_w.shape), _resident((1, SSD_CONV_DIM)),
        ],
        out_specs=[pl.BlockSpec((ROW_TILE, n), row) for n in widths],
        compiler_params=_params(("parallel",)),
        name="ssd_in_proj",
    )(h, h, h, mods_l, g.reshape(1, D_MODEL), wz, wx, wdt, conv_w, conv_b.reshape(1, SSD_CONV_DIM))


def _split_hi_lo(x):
    hi = x.astype(BF16)
    lo = (x - hi.astype(F32)).astype(BF16)
    return jnp.concatenate([hi, lo], axis=1)


def _ssd_prep_kernel(dt_ref, bias_ref, alog_ref, cs2_ref, rowt_ref, ws_ref):
    q = SSD_CHUNK
    ii = lax.broadcasted_iota(jnp.int32, (q, q), 0)
    jj = lax.broadcasted_iota(jnp.int32, (q, q), 1)
    lower = (jj <= ii).astype(F32)
    upper = (jj >= ii).astype(F32)
    is_fwd = lax.broadcasted_iota(jnp.int32, (q, LANE), 1) < SSD_HEADS
    neg_a = -jnp.exp(alog_ref[...])
    for c in range(PREP_TILE // q):
        rows = slice(c * q, (c + 1) * q)
        x_raw = dt_ref[rows, :] + bias_ref[...]
        dt = jnp.maximum(x_raw, 0.0) + jnp.log1p(jnp.exp(-jnp.abs(x_raw)))
        da = dt * neg_a
        cs_f = jnp.dot(lower, da, precision=lax.Precision.HIGHEST, preferred_element_type=F32)
        cs_b = jnp.dot(upper, da, precision=lax.Precision.HIGHEST, preferred_element_type=F32)
        cs = jnp.where(is_fwd, cs_f, cs_b)
        total = jnp.where(is_fwd[0:1, :], cs_f[q - 1:q, :], cs_b[0:1, :])
        cs2 = cs * LOG2E
        cs2_ref[rows, :] = cs2
        rowt_ref[rows, :] = (cs2 - jnp.log2(dt)).T
        w_end = dt * jnp.exp(total - cs)
        chunk_decay = jnp.broadcast_to(jnp.exp(total), (2 * SUBLANE, LANE))
        ws_ref[c] = _split_hi_lo(jnp.concatenate([w_end, chunk_decay], axis=0))


def _ssd_prep(dt_raw, bias_pad, alog_pad):
    q = SSD_CHUNK
    per = PREP_TILE // q
    row = lambda i: (i, 0)
    const = lambda i: (0, 0)
    return pl.pallas_call(
        _ssd_prep_kernel,
        out_shape=[jax.ShapeDtypeStruct((N_TOK, LANE), F32), jax.ShapeDtypeStruct((N_TOK, LANE), F32),
                   jax.ShapeDtypeStruct((N_TOK // q, q + 2 * SUBLANE, 2 * LANE), BF16)],
        grid=(N_TOK // PREP_TILE,),
        in_specs=[pl.BlockSpec((PREP_TILE, LANE), row), pl.BlockSpec((1, LANE), const),
                  pl.BlockSpec((1, LANE), const)],
        out_specs=[pl.BlockSpec((PREP_TILE, LANE), row), pl.BlockSpec((PREP_TILE, LANE), row),
                   pl.BlockSpec((per, q + 2 * SUBLANE, 2 * LANE), lambda i: (i, 0, 0))],
        compiler_params=_params(("parallel",)),
        name="ssd_prep",
    )(dt_raw, bias_pad, alog_pad)


def _ssd_scan_kernel(xbc_ref, cs2_ref, rowt_ref, ws_ref, e2_ref, *rest, reverse):
    if reverse:
        yf_ref, dexp_ref, o_ref, st_ref, lhs_ref = rest
    else:
        o_ref, st_ref, lhs_ref = rest
    q = SSD_CHUNK
    off = SSD_HEADS if reverse else 0

    @pl.when(pl.program_id(1) == 0)
    def _():
        st_ref[...] = jnp.zeros(st_ref.shape, F32)

    ii = lax.broadcasted_iota(jnp.int32, (q, q), 0)
    jj = lax.broadcasted_iota(jnp.int32, (q, q), 1)
    tri = (jj >= ii) if reverse else (jj <= ii)
    lane = lax.broadcasted_iota(jnp.int32, (q, LANE), 1)

    pairs = SSD_GROUP_W // LANE

    def build(c, g):
        r0 = c * q
        rows = slice(r0, r0 + q)
        b_g = xbc_ref[rows, SSD_INNER + g * SSD_STATE:SSD_INNER + (g + 1) * SSD_STATE]
        c_lo = SSD_INNER + SSD_BC + g * SSD_STATE
        c_g = xbc_ref[rows, c_lo:c_lo + SSD_STATE]
        cb = _dot_nt(c_g.astype(BF16), b_g.astype(BF16))
        for k in range(pairs):
            buf = lhs_ref.at[c, g * pairs + k]
            for hh, h in enumerate((g * SSD_HPG + 2 * k, g * SSD_HPG + 2 * k + 1)):
                col = off + h
                cs_i = jnp.broadcast_to(cs2_ref[rows, col:col + 1], (q, q))
                row_j = rowt_ref[r0 + col:r0 + col + 1, :]
                m = jnp.exp2(jnp.where(tri, cs_i - row_j, -jnp.inf)) * cb
                c_scaled = c_g * jnp.exp2(cs_i)
                buf[hh * q:(hh + 1) * q, 0:q] = m.astype(BF16)
                buf[hh * q:(hh + 1) * q, q:2 * q] = c_scaled.astype(BF16)

    def apply(c, g, spread):
        rows = slice(c * q, (c + 1) * q)
        b_g = xbc_ref[rows, SSD_INNER + g * SSD_STATE:SSD_INNER + (g + 1) * SSD_STATE]
        gs = slice(g * SSD_GROUP_W, (g + 1) * SSD_GROUP_W)
        st_old = st_ref[:, gs]
        st_bf = st_old.astype(BF16)
        x_g = xbc_ref[rows, gs]
        for k in range(pairs):
            ls = slice(k * LANE, (k + 1) * LANE)
            rhs = jnp.concatenate([x_g[:, ls].astype(BF16), st_bf[:, ls]], axis=0)
            out = _dot(lhs_ref[c, g * pairs + k], rhs)
            y_pair = jnp.where(lane < SSD_HEADDIM, out[0:q, :], out[q:2 * q, :])
            cols = slice(g * SSD_GROUP_W + k * LANE, g * SSD_GROUP_W + (k + 1) * LANE)
            if reverse:
                y_pair = y_pair + yf_ref[rows, cols].astype(F32) + dexp_ref[:, cols] * x_g[:, ls]
            o_ref[rows, cols] = y_pair.astype(o_ref.dtype)
        xw = (x_g * spread[0:q, gs]).astype(BF16)
        st_ref[:, gs] = st_old * spread[q:q + 1, gs] + _dot(b_g.T.astype(BF16), xw)

    order = list(range(SCAN_CHUNKS - 1, -1, -1) if reverse else range(SCAN_CHUNKS))
    for g in range(SSD_GROUPS):
        build(order[0], g)
    for n, c in enumerate(order):
        spread = _dot(ws_ref[c], e2_ref[:, 0:SSD_INNER])
        for g in range(SSD_GROUPS):
            if n + 1 < len(order):
                build(order[n + 1], g)
            apply(c, g, spread)


def _ssd_scan(xbc_act, cs2, rowt, ws, e2, reverse, extra=()):
    block = SCAN_CHUNKS * SSD_CHUNK
    lat_steps = SEQ // block
    ctx_base = N_LAT // block
    steps = 1 + lat_steps

    def blk(b, s):
        lat = (lat_steps - s) if reverse else (s - 1)
        return jnp.where(s == 0, ctx_base + b, b * lat_steps + lat)

    rows = lambda w: pl.BlockSpec((block, w), lambda b, s: (blk(b, s), 0))
    const = lambda shape: pl.BlockSpec(shape, lambda b, s: (0, 0))
    ws_spec = pl.BlockSpec((SCAN_CHUNKS,) + ws.shape[1:], lambda b, s: (blk(b, s), 0, 0))
    in_specs = [rows(SSD_CONV_DIM), rows(LANE), rows(LANE), ws_spec, const(e2.shape)]
    args = [xbc_act, cs2, rowt, ws, e2]
    if reverse:
        y_f, d_exp = extra
        in_specs += [rows(SSD_INNER), const((1, SSD_INNER))]
        args += [y_f, d_exp]
    scratch = [pltpu.VMEM((SSD_STATE, SSD_INNER), F32),
               pltpu.VMEM((SCAN_CHUNKS, SSD_HEADS // 2, 2 * SSD_CHUNK, 2 * SSD_CHUNK), BF16)]
    return pl.pallas_call(
        functools.partial(_ssd_scan_kernel, reverse=reverse),
        out_shape=jax.ShapeDtypeStruct((N_TOK, SSD_INNER), BF16),
        grid=(BATCH, steps),
        in_specs=in_specs,
        out_specs=rows(SSD_INNER),
        scratch_shapes=scratch,
        compiler_params=_params(("parallel", "arbitrary")),
        name="ssd_scan_bwd" if reverse else "ssd_scan_fwd",
    )(*args)


def _head_spread_matrix(off):
    e = np.zeros((LANE, SSD_INNER), np.float32)
    for h in range(SSD_HEADS):
        e[off + h, h * SSD_HEADDIM:(h + 1) * SSD_HEADDIM] = 1.0
    return _mxu_weight(jnp.asarray(np.concatenate([e, e], axis=0)))


def _ssd_mixer(h, mods_l, g, w_in, conv_w, conv_b, dt_bias, a_log, d_skip, norm_w, w_out):
    wz = _mxu_weight(w_in[:, :SSD_INNER])
    wxbc = _mxu_weight(w_in[:, SSD_INNER:SSD_INNER + SSD_CONV_DIM])
    n_dt = 2 * SSD_HEADS
    wdt = _mxu_weight(w_in[:, SSD_INNER + SSD_CONV_DIM:])
    z, xbc_act, dt_raw = _ssd_in(h, mods_l, g, wz, wxbc, wdt, conv_w, conv_b)
    bias_pad = jnp.pad(dt_bias.reshape(1, n_dt), ((0, 0), (0, LANE - n_dt)))
    alog_pad = jnp.pad(a_log.reshape(1, n_dt), ((0, 0), (0, LANE - n_dt)))
    cs2, rowt, ws = _ssd_prep(dt_raw, bias_pad, alog_pad)
    y_f = _ssd_scan(xbc_act, cs2, rowt, ws, _head_spread_matrix(0), False)
    d_exp = jnp.repeat(d_skip, SSD_HEADDIM).reshape(1, SSD_INNER)
    y = _ssd_scan(xbc_act, cs2, rowt, ws, _head_spread_matrix(SSD_HEADS), True,
                  extra=(y_f, d_exp))
    return h, (y, z, norm_w, w_out.astype(BF16))


def _shortconv_kernel(h_ref, mod_ref, g_ref, wb_ref, wc_ref, wh_ref, cw_ref, wo_ref, o_ref, v_ref):
    h = h_ref[...]
    u = _norm_mod(h, g_ref[...], mod_ref[3:4, :], mod_ref[4:5, :]).astype(BF16)
    v = _dot(u, wc_ref[:, 0:D_MODEL]) * _dot(u, wh_ref[:, 0:D_MODEL])
    v_ref[0:HALO, :] = jnp.zeros((HALO, D_MODEL), F32)
    v_ref[HALO:HALO + ROW_TILE, :] = v
    v_ref[HALO + ROW_TILE:, :] = jnp.zeros((HALO, D_MODEL), F32)
    is_ctx = pl.program_id(0) >= N_LAT // ROW_TILE
    row_len = jnp.where(is_ctx, CTX_LEN, GRID_W)
    pos = lax.broadcasted_iota(jnp.int32, (ROW_TILE, 1), 0) & (row_len - 1)
    v_prev = jnp.where(pos == 0, 0.0, v_ref[HALO - 1:HALO - 1 + ROW_TILE, :])
    v_next = jnp.where(pos == row_len - 1, 0.0, v_ref[HALO + 1:HALO + 1 + ROW_TILE, :])
    conv = cw_ref[0:1, :] * v_prev + cw_ref[1:2, :] * v + cw_ref[2:3, :] * v_next
    y = _dot((_dot(u, wb_ref[:, 0:D_MODEL]) * conv).astype(BF16), wo_ref[:, 0:D_MODEL])
    o_ref[...] = h + mod_ref[5:6, :] * y


def _shortconv_mixer(h, mods_l, g, w_in, conv_w, w_out):
    wb, wc, wh = (w_in[:, k * D_MODEL:(k + 1) * D_MODEL].astype(BF16) for k in range(3))
    row = lambda i: (i, 0)
    const = lambda i: (0, 0)
    wspec = _resident(wb.shape)
    return pl.pallas_call(
        _shortconv_kernel,
        out_shape=jax.ShapeDtypeStruct((N_TOK, D_MODEL), F32),
        grid=(N_TOK // ROW_TILE,),
        in_specs=[
            pl.BlockSpec((ROW_TILE, D_MODEL), row),
            pl.BlockSpec((None, N_MOD, D_MODEL), lambda i: (_mod_row(i, ROW_TILE), 0, 0)),
            pl.BlockSpec((1, D_MODEL), const),
            wspec, wspec, wspec,
            pl.BlockSpec(conv_w.shape, const),
            wspec,
        ],
        out_specs=pl.BlockSpec((ROW_TILE, D_MODEL), row),
        scratch_shapes=[pltpu.VMEM((ROW_TILE + 2 * HALO, D_MODEL), F32)],
        compiler_params=_params(("parallel",)),
        name="shortconv_mixer",
    )(h, mods_l, g.reshape(1, D_MODEL), wb, wc, wh, conv_w, w_out.astype(BF16))


def _gmlp_kernel(h_ref, mod_ref, g_ref, wu_ref, wv_ref, vn_ref, ws_ref, bs_ref, wo_ref, o_ref, a_ref):
    h = h_ref[...]
    u = _norm_mod(h, g_ref[...], mod_ref[3:4, :], mod_ref[4:5, :]).astype(BF16)
    zu = jax.nn.gelu(_dot(u, wu_ref[:, 0:GM_DIM]), approximate=True)
    zv = jax.nn.gelu(_dot(u, wv_ref[:, 0:GM_DIM]), approximate=True)
    zc = zv - jnp.mean(zv, axis=-1, keepdims=True)
    zv = zc * lax.rsqrt(jnp.mean(zc * zc, axis=-1, keepdims=True) + EPS) * vn_ref[...]
    zv = zv.astype(BF16)
    for c in range(ROW_TILE // GM_CHUNK):
        rs = slice(c * GM_CHUNK, (c + 1) * GM_CHUNK)
        for g in range(GM_GROUPS):
            gs = slice(g * GM_GROUP_DIM, (g + 1) * GM_GROUP_DIM)
            s = _dot(ws_ref[g], zv[rs, gs]) + bs_ref[:, g:g + 1]
            a_ref[rs, gs] = (zu[rs, gs] * s).astype(BF16)
    o_ref[...] = h + mod_ref[5:6, :] * _dot(a_ref[...], wo_ref[:, 0:D_MODEL])


def _gmlp_mixer(h, mods_l, g, w_in, v_norm, w_s, b_s, w_out):
    wu = _mxu_weight(w_in[:, :GM_DIM])
    wv = _mxu_weight(w_in[:, GM_DIM:])
    wo = _mxu_weight(w_out)
    row = lambda i: (i, 0)
    const = lambda i: (0, 0)
    return pl.pallas_call(
        _gmlp_kernel,
        out_shape=jax.ShapeDtypeStruct((N_TOK, D_MODEL), F32),
        grid=(N_TOK // ROW_TILE,),
        in_specs=[
            pl.BlockSpec((ROW_TILE, D_MODEL), row),
            pl.BlockSpec((None, N_MOD, D_MODEL), lambda i: (_mod_row(i, ROW_TILE), 0, 0)),
            pl.BlockSpec((1, D_MODEL), const),
            _resident(wu.shape),
            _resident(wv.shape),
            pl.BlockSpec((1, GM_DIM), const),
            pl.BlockSpec(w_s.shape, lambda i: (0, 0, 0)),
            pl.BlockSpec((GM_CHUNK, GM_GROUPS), const),
            _resident(wo.shape),
        ],
        out_specs=pl.BlockSpec((ROW_TILE, D_MODEL), row),
        scratch_shapes=[pltpu.VMEM((ROW_TILE, GM_DIM), BF16)],
        compiler_params=_params(("parallel",)),
        name="gmlp_mixer",
    )(h, mods_l, g.reshape(1, D_MODEL), wu, wv, v_norm.reshape(1, GM_DIM), w_s.astype(BF16),
      b_s.T, wo)


def kernel(x, c, ctx, c_ctx, ada_w, ada_b, norm_g, ffn_wg, ffn_wu, ffn_wd, ssd_in, ssd_conv_w, ssd_conv_b, ssd_dt_bias, ssd_a_log, ssd_d, ssd_norm, ssd_out, sc_in, sc_conv, sc_out, gm_in, gm_vnorm, gm_ws, gm_bs, gm_out, final_norm):
    h = (x.reshape(N_LAT, D_MODEL), ctx.reshape(N_CTX, D_MODEL))
    mods = _modulations(c, c_ctx, ada_w, ada_b)
    ffn_w = (ffn_wg.astype(BF16), ffn_wu.astype(BF16), ffn_wd.astype(BF16))
    for i in range(DEPTH):
        last = i == DEPTH - 1
        m = mods[i]
        h = _half_ffn(h, m, norm_g[i, 0], ffn_w, i, 0, N_TOK)
        kind, j = i % 3, i // 3
        tail = None
        if kind == 0:
            h, tail = _ssd_mixer(h, m, norm_g[i, 1], ssd_in[j], ssd_conv_w[j], ssd_conv_b[j], ssd_dt_bias[j],
                           ssd_a_log[j], ssd_d[j], ssd_norm[j], ssd_out[j])
        elif kind == 1:
            h = _shortconv_mixer(h, m, norm_g[i, 1], sc_in[j], sc_conv[j], sc_out[j])
        else:
            h = _gmlp_mixer(h, m, norm_g[i, 1], gm_in[j], gm_vnorm[j], gm_ws[j], gm_bs[j], gm_out[j])
        h = _half_ffn(h, m, norm_g[i, 2], ffn_w, i, 1, N_LAT if last else N_TOK,
                      final_g=final_norm if last else None, ssd_tail=tail)
    return h.reshape(BATCH, SEQ, D_MODEL)
```

```python
import functools

import jax
import jax.numpy as jnp
import numpy as np
from jax import lax
from jax.experimental import pallas as pl
from jax.experimental.pallas import tpu as pltpu

F32 = jnp.float32
BF16 = jnp.bfloat16

D_MODEL = 1024
BATCH = 2
SEQ = 8192
DEPTH = 4
GRID_W = 64
CTX_LEN = 256
N_MOD = 9
EPS = 1e-6
LOG2E = 1.4426950408889634
FFN_DIM = 2816

SSD_INNER = 2048
SSD_HEADDIM = 64
SSD_HEADS = 32
SSD_STATE = 128
SSD_GROUPS = 4
SSD_HPG = 8
SSD_CONV = 5
SSD_CHUNK = 128
SSD_BC = SSD_GROUPS * SSD_STATE
SSD_CONV_DIM = SSD_INNER + 2 * SSD_BC
SSD_GROUP_W = SSD_HPG * SSD_HEADDIM

GM_CHUNK = 128
GM_DIM = 2048
GM_GROUPS = 8
GM_GROUP_DIM = 256

N_LAT = BATCH * SEQ
N_CTX = BATCH * CTX_LEN
N_TOK = N_LAT + N_CTX
CTX_MOD_ROW = BATCH

LANE = 128
SUBLANE = 8
VMEM_LIMIT = 56 * 1024 * 1024

FFN_TM = 512
FFN_FC = 256
ROW_TILE = 256
PREP_TILE = 512
HALO = SUBLANE
CONV_COLS = 256
SCAN_CHUNKS = 2
CONV_GROUPS = 16


def _dot(a, b):
    return jnp.dot(a, b, preferred_element_type=F32)


def _dot_nt(a, b):
    return lax.dot_general(a, b, (((1,), (1,)), ((), ())), preferred_element_type=F32)


def _mxu_weight(w):
    tiles = -(-w.shape[1] // LANE)
    pad = (tiles + (tiles + 1) % 2) * LANE - w.shape[1]
    return jnp.pad(w.astype(BF16), ((0, 0), (0, pad)))


def _silu(x):
    hx = 0.5 * x
    return hx + hx * jnp.tanh(hx)


def _mod_row(tile, tile_rows):
    return jnp.minimum((tile * tile_rows) // SEQ, CTX_MOD_ROW)


def _norm_mod(h, g, shift, scale):
    y = h * lax.rsqrt(jnp.mean(h * h, axis=-1, keepdims=True) + EPS) * g
    return y * (1.0 + scale) + shift


def _params(sem):
    return pltpu.CompilerParams(dimension_semantics=sem, vmem_limit_bytes=VMEM_LIMIT)


def _mod_kernel(ct_ref, w_ref, b_ref, o_ref, s_ref):
    n_vec = BATCH + 1
    tn = w_ref.shape[-1]

    @pl.when(jnp.logical_and(pl.program_id(0) == 0, pl.program_id(1) == 0))
    def _():
        s = _silu(ct_ref[...])
        for r in range(n_vec):
            s_ref[r] = jnp.broadcast_to(s[:, r:r + 1], (D_MODEL, LANE))

    def body(k, accs):
        r0 = pl.multiple_of(k * SUBLANE, SUBLANE)
        w8 = w_ref[pl.ds(r0, SUBLANE), :]
        out = []
        for r, a in enumerate(accs):
            s8 = s_ref[r, pl.ds(r0, SUBLANE), :]
            out.append(a + w8 * jnp.concatenate([s8] * (tn // LANE), axis=1))
        return tuple(out)

    zero = jnp.zeros((SUBLANE, tn), F32)
    accs = lax.fori_loop(0, D_MODEL // SUBLANE, body, (zero,) * n_vec, unroll=4)
    o_ref[...] = jnp.zeros(o_ref.shape, F32)
    for r, a in enumerate(accs):
        o_ref[r:r + 1, :] = jnp.sum(a, axis=0, keepdims=True) + b_ref[...]


def _modulations(c, c_ctx, ada_w, ada_b):
    ct = jnp.concatenate([c, c_ctx[None, :], jnp.zeros((SUBLANE - BATCH - 1, D_MODEL), F32)], axis=0).T
    tn = 1536
    n_out = N_MOD * D_MODEL
    out = pl.pallas_call(
        _mod_kernel,
        out_shape=jax.ShapeDtypeStruct((DEPTH, SUBLANE, n_out), F32),
        grid=(DEPTH, n_out // tn),
        in_specs=[
            pl.BlockSpec((D_MODEL, SUBLANE), lambda l, j: (0, 0)),
            pl.BlockSpec((None, D_MODEL, tn), lambda l, j: (l, 0, j)),
            pl.BlockSpec((None, 1, tn), lambda l, j: (l, 0, j)),
        ],
        out_specs=pl.BlockSpec((None, SUBLANE, tn), lambda l, j: (l, 0, j)),
        scratch_shapes=[pltpu.VMEM((BATCH + 1, D_MODEL, LANE), F32)],
        compiler_params=_params(("arbitrary", "arbitrary")),
        name="adaln_mod",
    )(ct, ada_w, ada_b.reshape(DEPTH, 1, n_out))
    return out.reshape(DEPTH, SUBLANE, N_MOD, D_MODEL)


def _ffn_kernel(*refs, mod_base, final, split_input, ssd_tail):
    if split_input:
        x_ref, ctx_ref, *refs = refs
        is_ctx = pl.program_id(0) >= N_LAT // FFN_TM
        h = jnp.where(is_ctx, ctx_ref[...], x_ref[...])
    else:
        h_ref, *refs = refs
        h = h_ref[...]
    if ssd_tail:
        y_ref, z_ref, nw_ref, wo_ref, *refs = refs
    mod_ref, g_ref, wg_ref, wu_ref, wd_ref, *rest = refs
    if final:
        fg_ref, o_ref, a_ref, *rest = rest
    else:
        o_ref, a_ref, *rest = rest
    if ssd_tail:
        ya_ref, = rest
        for g in range(SSD_GROUPS):
            gs = slice(g * SSD_GROUP_W, (g + 1) * SSD_GROUP_W)
            y = y_ref[:, gs].astype(F32) * _silu(z_ref[:, gs].astype(F32))
            y = y * lax.rsqrt(jnp.mean(y * y, axis=-1, keepdims=True) + EPS) * nw_ref[:, gs]
            ya_ref[:, gs] = y.astype(BF16)
        h = h + mod_ref[5:6, :] * _dot(ya_ref[...], wo_ref[:, 0:D_MODEL])
    u = _norm_mod(h, g_ref[...], mod_ref[mod_base:mod_base + 1, :],
                  mod_ref[mod_base + 1:mod_base + 2, :]).astype(BF16)
    for c0 in range(0, FFN_DIM, FFN_FC):
        cs = slice(c0, c0 + FFN_FC)
        a_ref[:, cs] = (_silu(_dot(u, wg_ref[:, cs])) * _dot(u, wu_ref[:, cs])).astype(BF16)
    out = h + 0.5 * mod_ref[mod_base + 2:mod_base + 3, :] * _dot(a_ref[...], wd_ref[:, 0:D_MODEL])
    if final:
        out = out * lax.rsqrt(jnp.mean(out * out, axis=-1, keepdims=True) + EPS) * fg_ref[...]
    o_ref[...] = out


def _resident(shape):
    return pl.BlockSpec(shape, lambda *_: (0,) * len(shape), pipeline_mode=pl.Buffered(1))


def _half_ffn(h, mods_l, g, weights, layer, half, n_rows, final_g=None, ssd_tail=None):
    final = final_g is not None
    split_input = isinstance(h, tuple)
    mod_base = 6 * half
    row = lambda i: (i, 0)
    if split_input:
        lat_tiles = N_LAT // FFN_TM
        h_specs = [pl.BlockSpec((FFN_TM, D_MODEL), lambda i: (jnp.minimum(i, lat_tiles - 1), 0)),
                   pl.BlockSpec((FFN_TM, D_MODEL), lambda i: (jnp.maximum(i - lat_tiles, 0), 0))]
        h_args = list(h)
    else:
        h_specs, h_args = [pl.BlockSpec((FFN_TM, D_MODEL), row)], [h]
    w_spec = lambda w: pl.BlockSpec((None, None) + w.shape[2:], lambda i: (layer, half, 0, 0),
                                    pipeline_mode=pl.Buffered(1))
    scratch = [pltpu.VMEM((FFN_TM, FFN_DIM), BF16)]
    if ssd_tail is not None:
        y, z, norm_w, w_out = ssd_tail
        h_specs += [pl.BlockSpec((FFN_TM, SSD_INNER), row), pl.BlockSpec((FFN_TM, SSD_INNER), row),
                    _resident((1, SSD_INNER)), _resident(w_out.shape)]
        h_args += [y, z, norm_w.reshape(1, SSD_INNER), w_out]
        scratch.append(pltpu.VMEM((FFN_TM, SSD_INNER), BF16))
    in_specs = h_specs + [
        pl.BlockSpec((None, N_MOD, D_MODEL), lambda i: (_mod_row(i, FFN_TM), 0, 0)),
        _resident((1, D_MODEL)),
    ] + [w_spec(w) for w in weights]
    args = h_args + [mods_l, g.reshape(1, D_MODEL)] + list(weights)
    if final:
        in_specs.append(_resident((1, D_MODEL)))
        args.append(final_g.reshape(1, D_MODEL))
    return pl.pallas_call(
        functools.partial(_ffn_kernel, mod_base=mod_base, final=final, split_input=split_input,
                          ssd_tail=ssd_tail is not None),
        out_shape=jax.ShapeDtypeStruct((n_rows, D_MODEL), F32),
        grid=(n_rows // FFN_TM,),
        in_specs=in_specs,
        out_specs=pl.BlockSpec((FFN_TM, D_MODEL), row),
        scratch_shapes=scratch,
        compiler_params=_params(("parallel",)),
        name=("ssd_tail_" if ssd_tail is not None else "") + ("half_ffn_final" if final else "half_ffn"),
    )(*args)


def _seq_edges(tile):
    lat_tiles = SEQ // ROW_TILE
    is_ctx = tile >= BATCH * lat_tiles
    pos = tile % lat_tiles
    return jnp.logical_or(is_ctx, pos == 0), jnp.logical_or(is_ctx, pos == lat_tiles - 1)


def _ssd_in_kernel(hp_ref, h_ref, hn_ref, mod_ref, g_ref, wz_ref, wx_ref, wdt_ref, cw_ref, cb_ref,
                   z_ref, x_ref, dt_ref):
    first, last = _seq_edges(pl.program_id(0))
    g, shift, scale = g_ref[...], mod_ref[3:4, :], mod_ref[4:5, :]
    u = _norm_mod(h_ref[...], g, shift, scale)
    u_ext = jnp.concatenate([_norm_mod(hp_ref[...], g, shift, scale), u,
                             _norm_mod(hn_ref[...], g, shift, scale)], axis=0).astype(BF16)
    ub = u.astype(BF16)
    n_grp = (ROW_TILE + 2 * HALO) // SUBLANE
    half = SSD_CONV // 2
    sub = lax.broadcasted_iota(jnp.int32, (1, SUBLANE, 1), 1)

    def project(blk):
        cols = slice(blk * CONV_COLS, (blk + 1) * CONV_COLS)
        e = _dot(u_ext, wx_ref[:, cols])
        e = jnp.concatenate([jnp.where(first, 0.0, e[0:HALO, :]), e[HALO:HALO + ROW_TILE, :],
                             jnp.where(last, 0.0, e[HALO + ROW_TILE:, :])], axis=0)
        return e.reshape(n_grp, SUBLANE, CONV_COLS)

    def conv(blk, e3):
        cols = slice(blk * CONV_COLS, (blk + 1) * CONV_COLS)
        for g0 in range(0, ROW_TILE // SUBLANE, CONV_GROUPS):
            ein = e3[g0:g0 + CONV_GROUPS + 2]
            acc = cb_ref[:, cols] + cw_ref[half:half + 1, cols] * ein[1:CONV_GROUPS + 1]
            for k in range(SSD_CONV):
                s = k - half
                if s == 0:
                    continue
                r3 = pltpu.roll(ein, (-s) % SUBLANE, axis=1)
                if s > 0:
                    shifted = jnp.where(sub < SUBLANE - s, r3[1:CONV_GROUPS + 1], r3[2:CONV_GROUPS + 2])
                else:
                    shifted = jnp.where(sub >= -s, r3[1:CONV_GROUPS + 1], r3[0:CONV_GROUPS])
                acc = acc + cw_ref[k:k + 1, cols] * shifted
            rows = slice(g0 * SUBLANE, (g0 + CONV_GROUPS) * SUBLANE)
            x_ref[rows, cols] = _silu(acc).reshape(CONV_GROUPS * SUBLANE, CONV_COLS).astype(x_ref.dtype)

    n_x = SSD_CONV_DIM // CONV_COLS
    n_z = SSD_INNER // CONV_COLS
    e3 = project(0)
    for blk in range(n_x):
        e3_next = project(blk + 1) if blk + 1 < n_x else None
        if blk < n_z:
            cols = slice(blk * CONV_COLS, (blk + 1) * CONV_COLS)
            z_ref[:, cols] = _dot(ub, wz_ref[:, cols]).astype(z_ref.dtype)
        elif blk == n_z:
            dt_ref[...] = _dot(ub, wdt_ref[...])
        conv(blk, e3)
        e3 = e3_next


def _ssd_in(h, mods_l, g, wz, wx, wdt, conv_w, conv_b):
    per = ROW_TILE // HALO
    n_halo = N_TOK // HALO
    row = lambda i: (i, 0)
    widths = (SSD_INNER, SSD_CONV_DIM, LANE)
    return pl.pallas_call(
        _ssd_in_kernel,
        out_shape=[jax.ShapeDtypeStruct((N_TOK, n), dt) for n, dt in zip(widths, (BF16, F32, F32))],
        grid=(N_TOK // ROW_TILE,),
        in_specs=[
            pl.BlockSpec((HALO, D_MODEL), lambda i: (jnp.maximum(i * per - 1, 0), 0)),
            pl.BlockSpec((ROW_TILE, D_MODEL), row),
            pl.BlockSpec((HALO, D_MODEL), lambda i: (jnp.minimum((i + 1) * per, n_halo - 1), 0)),
            pl.BlockSpec((None, N_MOD, D_MODEL), lambda i: (_mod_row(i, ROW_TILE), 0, 0)),
            _resident((1, D_MODEL)),
            _resident(wz.shape), _resident(wx.shape), _resident(wdt.shape),
            _resident(conv_w.shape), _resident((1, SSD_CONV_DIM)),
        ],
        out_specs=[pl.BlockSpec((ROW_TILE, n), row) for n in widths],
        compiler_params=_params(("parallel",)),
        name="ssd_in_proj",
    )(h, h, h, mods_l, g.reshape(1, D_MODEL), wz, wx, wdt, conv_w, conv_b.reshape(1, SSD_CONV_DIM))


def _split_hi_lo(x):
    hi = x.astype(BF16)
    lo = (x - hi.astype(F32)).astype(BF16)
    return jnp.concatenate([hi, lo], axis=1)


def _ssd_prep_kernel(dt_ref, bias_ref, alog_ref, cs2_ref, rowt_ref, ws_ref):
    q = SSD_CHUNK
    ii = lax.broadcasted_iota(jnp.int32, (q, q), 0)
    jj = lax.broadcasted_iota(jnp.int32, (q, q), 1)
    lower = (jj <= ii).astype(F32)
    upper = (jj >= ii).astype(F32)
    is_fwd = lax.broadcasted_iota(jnp.int32, (q, LANE), 1) < SSD_HEADS
    neg_a = -jnp.exp(alog_ref[...])
    for c in range(PREP_TILE // q):
        rows = slice(c * q, (c + 1) * q)
        x_raw = dt_ref[rows, :] + bias_ref[...]
        dt = jnp.maximum(x_raw, 0.0) + jnp.log1p(jnp.exp(-jnp.abs(x_raw)))
        da = dt * neg_a
        cs_f = jnp.dot(lower, da, precision=lax.Precision.HIGHEST, preferred_element_type=F32)
        cs_b = jnp.dot(upper, da, precision=lax.Precision.HIGHEST, preferred_element_type=F32)
        cs = jnp.where(is_fwd, cs_f, cs_b)
        total = jnp.where(is_fwd[0:1, :], cs_f[q - 1:q, :], cs_b[0:1, :])
        cs2 = cs * LOG2E
        cs2_ref[rows, :] = cs2
        rowt_ref[rows, :] = (cs2 - jnp.log2(dt)).T
        w_end = dt * jnp.exp(total - cs)
        chunk_decay = jnp.broadcast_to(jnp.exp(total), (2 * SUBLANE, LANE))
        ws_ref[c] = _split_hi_lo(jnp.concatenate([w_end, chunk_decay], axis=0))


def _ssd_prep(dt_raw, bias_pad, alog_pad):
    q = SSD_CHUNK
    per = PREP_TILE // q
    row = lambda i: (i, 0)
    const = lambda i: (0, 0)
    return pl.pallas_call(
        _ssd_prep_kernel,
        out_shape=[jax.ShapeDtypeStruct((N_TOK, LANE), F32), jax.ShapeDtypeStruct((N_TOK, LANE), F32),
                   jax.ShapeDtypeStruct((N_TOK // q, q + 2 * SUBLANE, 2 * LANE), BF16)],
        grid=(N_TOK // PREP_TILE,),
        in_specs=[pl.BlockSpec((PREP_TILE, LANE), row), pl.BlockSpec((1, LANE), const),
                  pl.BlockSpec((1, LANE), const)],
        out_specs=[pl.BlockSpec((PREP_TILE, LANE), row), pl.BlockSpec((PREP_TILE, LANE), row),
                   pl.BlockSpec((per, q + 2 * SUBLANE, 2 * LANE), lambda i: (i, 0, 0))],
        compiler_params=_params(("parallel",)),
        name="ssd_prep",
    )(dt_raw, bias_pad, alog_pad)


def _ssd_scan_kernel(xbc_ref, cs2_ref, rowt_ref, ws_ref, e2_ref, *rest, reverse):
    if reverse:
        yf_ref, dexp_ref, o_ref, st_ref, lhs_ref = rest
    else:
        o_ref, st_ref, lhs_ref = rest
    q = SSD_CHUNK
    off = SSD_HEADS if reverse else 0

    @pl.when(pl.program_id(1) == 0)
    def _():
        st_ref[...] = jnp.zeros(st_ref.shape, F32)

    ii = lax.broadcasted_iota(jnp.int32, (q, q), 0)
    jj = lax.broadcasted_iota(jnp.int32, (q, q), 1)
    tri = (jj >= ii) if reverse else (jj <= ii)
    lane = lax.broadcasted_iota(jnp.int32, (q, LANE), 1)

    pairs = SSD_GROUP_W // LANE

    def build(c, g):
        r0 = c * q
        rows = slice(r0, r0 + q)
        b_g = xbc_ref[rows, SSD_INNER + g * SSD_STATE:SSD_INNER + (g + 1) * SSD_STATE]
        c_lo = SSD_INNER + SSD_BC + g * SSD_STATE
        c_g = xbc_ref[rows, c_lo:c_lo + SSD_STATE]
        cb = _dot_nt(c_g.astype(BF16), b_g.astype(BF16))
        for k in range(pairs):
            buf = lhs_ref.at[c, g * pairs + k]
            for hh, h in enumerate((g * SSD_HPG + 2 * k, g * SSD_HPG + 2 * k + 1)):
                col = off + h
                cs_i = jnp.broadcast_to(cs2_ref[rows, col:col + 1], (q, q))
                row_j = rowt_ref[r0 + col:r0 + col + 1, :]
                m = jnp.exp2(jnp.where(tri, cs_i - row_j, -jnp.inf)) * cb
                c_scaled = c_g * jnp.exp2(cs_i)
                buf[hh * q:(hh + 1) * q, 0:q] = m.astype(BF16)
                buf[hh * q:(hh + 1) * q, q:2 * q] = c_scaled.astype(BF16)

    def apply(c, g, spread):
        rows = slice(c * q, (c + 1) * q)
        b_g = xbc_ref[rows, SSD_INNER + g * SSD_STATE:SSD_INNER + (g + 1) * SSD_STATE]
        gs = slice(g * SSD_GROUP_W, (g + 1) * SSD_GROUP_W)
        st_old = st_ref[:, gs]
        st_bf = st_old.astype(BF16)
        x_g = xbc_ref[rows, gs]
        for k in range(pairs):
            ls = slice(k * LANE, (k + 1) * LANE)
            rhs = jnp.concatenate([x_g[:, ls].astype(BF16), st_bf[:, ls]], axis=0)
            out = _dot(lhs_ref[c, g * pairs + k], rhs)
            y_pair = jnp.where(lane < SSD_HEADDIM, out[0:q, :], out[q:2 * q, :])
            cols = slice(g * SSD_GROUP_W + k * LANE, g * SSD_GROUP_W + (k + 1) * LANE)
            if reverse:
                y_pair = y_pair + yf_ref[rows, cols].astype(F32) + dexp_ref[:, cols] * x_g[:, ls]
            o_ref[rows, cols] = y_pair.astype(o_ref.dtype)
        xw = (x_g * spread[0:q, gs]).astype(BF16)
        st_ref[:, gs] = st_old * spread[q:q + 1, gs] + _dot(b_g.T.astype(BF16), xw)

    order = list(range(SCAN_CHUNKS - 1, -1, -1) if reverse else range(SCAN_CHUNKS))
    for g in range(SSD_GROUPS):
        build(order[0], g)
    for n, c in enumerate(order):
        spread = _dot(ws_ref[c], e2_ref[:, 0:SSD_INNER])
        for g in range(SSD_GROUPS):
            if n + 1 < len(order):
                build(order[n + 1], g)
            apply(c, g, spread)


def _ssd_scan(xbc_act, cs2, rowt, ws, e2, reverse, extra=()):
    block = SCAN_CHUNKS * SSD_CHUNK
    lat_steps = SEQ // block
    ctx_base = N_LAT // block
    steps = 1 + lat_steps

    def blk(b, s):
        lat = (lat_steps - s) if reverse else (s - 1)
        return jnp.where(s == 0, ctx_base + b, b * lat_steps + lat)

    rows = lambda w: pl.BlockSpec((block, w), lambda b, s: (blk(b, s), 0))
    const = lambda shape: pl.BlockSpec(shape, lambda b, s: (0, 0))
    ws_spec = pl.BlockSpec((SCAN_CHUNKS,) + ws.shape[1:], lambda b, s: (blk(b, s), 0, 0))
    in_specs = [rows(SSD_CONV_DIM), rows(LANE), rows(LANE), ws_spec, const(e2.shape)]
    args = [xbc_act, cs2, rowt, ws, e2]
    if reverse:
        y_f, d_exp = extra
        in_specs += [rows(SSD_INNER), const((1, SSD_INNER))]
        args += [y_f, d_exp]
    scratch = [pltpu.VMEM((SSD_STATE, SSD_INNER), F32),
               pltpu.VMEM((SCAN_CHUNKS, SSD_HEADS // 2, 2 * SSD_CHUNK, 2 * SSD_CHUNK), BF16)]
    return pl.pallas_call(
        functools.partial(_ssd_scan_kernel, reverse=reverse),
        out_shape=jax.ShapeDtypeStruct((N_TOK, SSD_INNER), BF16),
        grid=(BATCH, steps),
        in_specs=in_specs,
        out_specs=rows(SSD_INNER),
        scratch_shapes=scratch,
        compiler_params=_params(("parallel", "arbitrary")),
        name="ssd_scan_bwd" if reverse else "ssd_scan_fwd",
    )(*args)


def _head_spread_matrix(off):
    e = np.zeros((LANE, SSD_INNER), np.float32)
    for h in range(SSD_HEADS):
        e[off + h, h * SSD_HEADDIM:(h + 1) * SSD_HEADDIM] = 1.0
    return _mxu_weight(jnp.asarray(np.concatenate([e, e], axis=0)))


def _ssd_mixer(h, mods_l, g, w_in, conv_w, conv_b, dt_bias, a_log, d_skip, norm_w, w_out):
    wz = _mxu_weight(w_in[:, :SSD_INNER])
    wxbc = _mxu_weight(w_in[:, SSD_INNER:SSD_INNER + SSD_CONV_DIM])
    n_dt = 2 * SSD_HEADS
    wdt = _mxu_weight(w_in[:, SSD_INNER + SSD_CONV_DIM:])
    z, xbc_act, dt_raw = _ssd_in(h, mods_l, g, wz, wxbc, wdt, conv_w, conv_b)
    bias_pad = jnp.pad(dt_bias.reshape(1, n_dt), ((0, 0), (0, LANE - n_dt)))
    alog_pad = jnp.pad(a_log.reshape(1, n_dt), ((0, 0), (0, LANE - n_dt)))
    cs2, rowt, ws = _ssd_prep(dt_raw, bias_pad, alog_pad)
    y_f = _ssd_scan(xbc_act, cs2, rowt, ws, _head_spread_matrix(0), False)
    d_exp = jnp.repeat(d_skip, SSD_HEADDIM).reshape(1, SSD_INNER)
    y = _ssd_scan(xbc_act, cs2, rowt, ws, _head_spread_matrix(SSD_HEADS), True,
                  extra=(y_f, d_exp))
    return h, (y, z, norm_w, w_out.astype(BF16))


def _shortconv_kernel(h_ref, mod_ref, g_ref, wb_ref, wc_ref, wh_ref, cw_ref, wo_ref, o_ref, v_ref):
    h = h_ref[...]
    u = _norm_mod(h, g_ref[...], mod_ref[3:4, :], mod_ref[4:5, :]).astype(BF16)
    v = _dot(u, wc_ref[:, 0:D_MODEL]) * _dot(u, wh_ref[:, 0:D_MODEL])
    v_ref[0:HALO, :] = jnp.zeros((HALO, D_MODEL), F32)
    v_ref[HALO:HALO + ROW_TILE, :] = v
    v_ref[HALO + ROW_TILE:, :] = jnp.zeros((HALO, D_MODEL), F32)
    is_ctx = pl.program_id(0) >= N_LAT // ROW_TILE
    row_len = jnp.where(is_ctx, CTX_LEN, GRID_W)
    pos = lax.broadcasted_iota(jnp.int32, (ROW_TILE, 1), 0) & (row_len - 1)
    v_prev = jnp.where(pos == 0, 0.0, v_ref[HALO - 1:HALO - 1 + ROW_TILE, :])
    v_next = jnp.where(pos == row_len - 1, 0.0, v_ref[HALO + 1:HALO + 1 + ROW_TILE, :])
    conv = cw_ref[0:1, :] * v_prev + cw_ref[1:2, :] * v + cw_ref[2:3, :] * v_next
    y = _dot((_dot(u, wb_ref[:, 0:D_MODEL]) * conv).astype(BF16), wo_ref[:, 0:D_MODEL])
    o_ref[...] = h + mod_ref[5:6, :] * y


def _shortconv_mixer(h, mods_l, g, w_in, conv_w, w_out):
    wb, wc, wh = (w_in[:, k * D_MODEL:(k + 1) * D_MODEL].astype(BF16) for k in range(3))
    row = lambda i: (i, 0)
    const = lambda i: (0, 0)
    wspec = _resident(wb.shape)
    return pl.pallas_call(
        _shortconv_kernel,
        out_shape=jax.ShapeDtypeStruct((N_TOK, D_MODEL), F32),
        grid=(N_TOK // ROW_TILE,),
        in_specs=[
            pl.BlockSpec((ROW_TILE, D_MODEL), row),
            pl.BlockSpec((None, N_MOD, D_MODEL), lambda i: (_mod_row(i, ROW_TILE), 0, 0)),
            pl.BlockSpec((1, D_MODEL), const),
            wspec, wspec, wspec,
            pl.BlockSpec(conv_w.shape, const),
            wspec,
        ],
        out_specs=pl.BlockSpec((ROW_TILE, D_MODEL), row),
        scratch_shapes=[pltpu.VMEM((ROW_TILE + 2 * HALO, D_MODEL), F32)],
        compiler_params=_params(("parallel",)),
        name="shortconv_mixer",
    )(h, mods_l, g.reshape(1, D_MODEL), wb, wc, wh, conv_w, w_out.astype(BF16))


def _gmlp_kernel(h_ref, mod_ref, g_ref, wu_ref, wv_ref, vn_ref, ws_ref, bs_ref, wo_ref, o_ref, a_ref):
    h = h_ref[...]
    u = _norm_mod(h, g_ref[...], mod_ref[3:4, :], mod_ref[4:5, :]).astype(BF16)
    zu = jax.nn.gelu(_dot(u, wu_ref[:, 0:GM_DIM]), approximate=True)
    zv = jax.nn.gelu(_dot(u, wv_ref[:, 0:GM_DIM]), approximate=True)
    zc = zv - jnp.mean(zv, axis=-1, keepdims=True)
    zv = zc * lax.rsqrt(jnp.mean(zc * zc, axis=-1, keepdims=True) + EPS) * vn_ref[...]
    zv = zv.astype(BF16)
    for c in range(ROW_TILE // GM_CHUNK):
        rs = slice(c * GM_CHUNK, (c + 1) * GM_CHUNK)
        for g in range(GM_GROUPS):
            gs = slice(g * GM_GROUP_DIM, (g + 1) * GM_GROUP_DIM)
            s = _dot(ws_ref[g], zv[rs, gs]) + bs_ref[:, g:g + 1]
            a_ref[rs, gs] = (zu[rs, gs] * s).astype(BF16)
    o_ref[...] = h + mod_ref[5:6, :] * _dot(a_ref[...], wo_ref[:, 0:D_MODEL])


def _gmlp_mixer(h, mods_l, g, w_in, v_norm, w_s, b_s, w_out):
    wu = _mxu_weight(w_in[:, :GM_DIM])
    wv = _mxu_weight(w_in[:, GM_DIM:])
    wo = _mxu_weight(w_out)
    row = lambda i: (i, 0)
    const = lambda i: (0, 0)
    return pl.pallas_call(
        _gmlp_kernel,
        out_shape=jax.ShapeDtypeStruct((N_TOK, D_MODEL), F32),
        grid=(N_TOK // ROW_TILE,),
        in_specs=[
            pl.BlockSpec((ROW_TILE, D_MODEL), row),
            pl.BlockSpec((None, N_MOD, D_MODEL), lambda i: (_mod_row(i, ROW_TILE), 0, 0)),
            pl.BlockSpec((1, D_MODEL), const),
            _resident(wu.shape),
            _resident(wv.shape),
            pl.BlockSpec((1, GM_DIM), const),
            pl.BlockSpec(w_s.shape, lambda i: (0, 0, 0)),
            pl.BlockSpec((GM_CHUNK, GM_GROUPS), const),
            _resident(wo.shape),
        ],
        out_specs=pl.BlockSpec((ROW_TILE, D_MODEL), row),
        scratch_shapes=[pltpu.VMEM((ROW_TILE, GM_DIM), BF16)],
        compiler_params=_params(("parallel",)),
        name="gmlp_mixer",
    )(h, mods_l, g.reshape(1, D_MODEL), wu, wv, v_norm.reshape(1, GM_DIM), w_s.astype(BF16),
      b_s.T, wo)


def kernel(x, c, ctx, c_ctx, ada_w, ada_b, norm_g, ffn_wg, ffn_wu, ffn_wd, ssd_in, ssd_conv_w, ssd_conv_b, ssd_dt_bias, ssd_a_log, ssd_d, ssd_norm, ssd_out, sc_in, sc_conv, sc_out, gm_in, gm_vnorm, gm_ws, gm_bs, gm_out, final_norm):
    h = (x.reshape(N_LAT, D_MODEL), ctx.reshape(N_CTX, D_MODEL))
    mods = _modulations(c, c_ctx, ada_w, ada_b)
    ffn_w = (ffn_wg.astype(BF16), ffn_wu.astype(BF16), ffn_wd.astype(BF16))
    for i in range(DEPTH):
        last = i == DEPTH - 1
        m = mods[i]
        h = _half_ffn(h, m, norm_g[i, 0], ffn_w, i, 0, N_TOK)
        kind, j = i % 3, i // 3
        tail = None
        if kind == 0:
            h, tail = _ssd_mixer(h, m, norm_g[i, 1], ssd_in[j], ssd_conv_w[j], ssd_conv_b[j], ssd_dt_bias[j],
                           ssd_a_log[j], ssd_d[j], ssd_norm[j], ssd_out[j])
        elif kind == 1:
            h = _shortconv_mixer(h, m, norm_g[i, 1], sc_in[j], sc_conv[j], sc_out[j])
        else:
            h = _gmlp_mixer(h, m, norm_g[i, 1], gm_in[j], gm_vnorm[j], gm_ws[j], gm_bs[j], gm_out[j])
        h = _half_ffn(h, m, norm_g[i, 2], ffn_w, i, 1, N_LAT if last else N_TOK,
                      final_g=final_norm if last else None, ssd_tail=tail)
    return h.reshape(BATCH, SEQ, D_MODEL)
```
